```python
import jax, jax.numpy as jnp
from jax import lax
import numpy as np


D_MODEL = 1024
BATCH = 8
SEQ = 4096
DEPTH = 1

MIX_WIDTH = D_MODEL
ATTN_WIDTH = MIX_WIDTH // 2
ATTN_HEADS = 8
ATTN_HEAD_DIM = ATTN_WIDTH // ATTN_HEADS
DILATED_PATTERNS = ((128, 1), (512, 4), (2048, 16))
ATTN_BLOCK = 128
ATTN_PAD_UNIT = ATTN_BLOCK * 16
HGRN_WIDTH = MIX_WIDTH - ATTN_WIDTH
HGRN_EXPAND = 128
HGRN_HEADS = HGRN_WIDTH // HGRN_EXPAND
HGRN_FDIM = HGRN_EXPAND
HGRN_IDIM = HGRN_WIDTH // HGRN_HEADS
HGRN_CHUNK = 64
D_FF = 2816
CONV_WIDTH = 3
NORM_EPS = 1e-6
IN_SIZES = (ATTN_WIDTH, ATTN_WIDTH, ATTN_WIDTH, HGRN_WIDTH, HGRN_WIDTH, HGRN_WIDTH, HGRN_WIDTH)
IN_TOTAL = sum(IN_SIZES)
IN_SPLITS = [sum(IN_SIZES[:j + 1]) for j in range(len(IN_SIZES) - 1)]

kernel_name = 'hybrid_dilated_attn_hgrn2_convglu'


def rms_norm(x, g):
    xf = x.astype(jnp.float32)
    y = xf * lax.rsqrt(jnp.mean(xf * xf, axis=-1, keepdims=True) + NORM_EPS)
    return (y * g.astype(jnp.float32)).astype(x.dtype)


def dilated_branch(q, k, v, window, dilation):
    B, Sp, H, E = q.shape
    n_keys = window // dilation
    nb = Sp // (dilation * ATTN_BLOCK)
    shape = (B, nb, ATTN_BLOCK, dilation, H, E)
    qb, kb, vb = q.reshape(shape), k.reshape(shape), v.reshape(shape)

    def with_prev(t):
        prev = jnp.pad(t, ((0, 0), (1, 0), (0, 0), (0, 0), (0, 0), (0, 0)))[:, :-1]
        return jnp.concatenate([prev, t], axis=2)

    kw, vw = with_prev(kb), with_prev(vb)
    s = jnp.einsum('bnqrhe,bnkrhe->bnrhqk', qb, kw,
                   preferred_element_type=jnp.float32) * (E ** -0.5)
    qi = jnp.arange(ATTN_BLOCK)[:, None]
    kj = jnp.arange(2 * ATTN_BLOCK)[None, :]
    dist = qi + ATTN_BLOCK - kj
    blk = jnp.arange(nb)[:, None, None]
    valid = (dist >= 0) & (dist <= n_keys) & (blk * ATTN_BLOCK + kj - ATTN_BLOCK >= 0)
    s = jnp.where(valid[None, :, None, None], s, -jnp.inf)
    m = jnp.max(s, axis=-1)
    p = jnp.exp(s - m[..., None])
    l = jnp.sum(p, axis=-1)
    o = jnp.einsum('bnrhqk,bnkrhe->bnqrhe', p.astype(v.dtype), vw,
                   preferred_element_type=jnp.float32).reshape(B, Sp, H, E)
    m = m.transpose(0, 1, 4, 2, 3).reshape(B, Sp, H)
    l = l.transpose(0, 1, 4, 2, 3).reshape(B, Sp, H)
    return o, m, l


def dilated_attention(q, k, v):
    B, S, H, E = q.shape
    Sp = -(-S // ATTN_PAD_UNIT) * ATTN_PAD_UNIT
    pad = ((0, 0), (0, Sp - S), (0, 0), (0, 0))
    q, k, v = jnp.pad(q, pad), jnp.pad(k, pad), jnp.pad(v, pad)
    outs, maxes, sums = [], [], []
    for window, dilation in DILATED_PATTERNS:
        o, m, l = dilated_branch(q, k, v, window, dilation)
        outs.append(o)
        maxes.append(m)
        sums.append(l)
    ms, ls, os_ = jnp.stack(maxes), jnp.stack(sums), jnp.stack(outs)
    w = jnp.exp(ms - jnp.max(ms, axis=0, keepdims=True))
    den = jnp.sum(w * ls, axis=0)
    num = jnp.sum(w[..., None] * os_, axis=0)
    return (num / den[..., None])[:, :S]


def hgrn2_mixer(q, f, i, lb):
    B, S, H, K = q.shape
    V = i.shape[-1]
    C = HGRN_CHUNK
    nc = S // C
    qf = jax.nn.silu(q.astype(jnp.float32))
    forget = lb + (1.0 - lb) * jax.nn.sigmoid(f.astype(jnp.float32))
    key = 1.0 - forget
    log_f = jnp.log(forget)
    iv = i.astype(jnp.float32)
    qf = qf.reshape(B, nc, C, H, K)
    key = key.reshape(B, nc, C, H, K)
    log_f = log_f.reshape(B, nc, C, H, K)
    iv = iv.reshape(B, nc, C, H, V)
    b = jnp.cumsum(log_f, axis=2)
    q_dec = qf * jnp.exp(b)
    k_inv = key * jnp.exp(-b)
    A = jnp.einsum('bnthk,bnshk->bnhts', q_dec, k_inv)
    causal = jnp.tril(jnp.ones((C, C), dtype=bool))
    A = jnp.where(causal, A, 0.0)
    o_intra = jnp.einsum('bnhts,bnshv->bnthv', A, iv)
    b_end = b[:, :, -1]
    k_end = key * jnp.exp(b_end[:, :, None] - b)
    U = jnp.einsum('bnshk,bnshv->bnhkv', k_end, iv)
    decay = jnp.exp(b_end)

    def step(state, xs):
        d, u = xs
        return d[..., None] * state + u, state

    init = jnp.zeros((B, H, K, V), jnp.float32)
    _, states = lax.scan(step, init, (decay.transpose(1, 0, 2, 3), U.transpose(1, 0, 2, 3, 4)))
    states = states.transpose(1, 0, 2, 3, 4)
    o_inter = jnp.einsum('bnthk,bnhkv->bnthv', q_dec, states)
    return (o_intra + o_inter).reshape(B, S, H, V)


def conv_glu(u, w_up, conv_w, conv_b, w_down):
    S = u.shape[1]
    gate, val = jnp.split(u @ w_up, 2, axis=-1)
    gp = jnp.pad(gate, ((0, 0), (CONV_WIDTH - 1, 0), (0, 0)))
    conv = conv_b
    for j in range(CONV_WIDTH):
        conv = conv + conv_w[j] * gp[:, j:j + S]
    return (jax.nn.gelu(conv, approximate=False) * val) @ w_down


def setup_inputs(seed: int = 0) -> dict:
    key = jax.random.key(seed)
    ks = jax.random.split(key, 13)
    f32 = jnp.float32
    nrm = lambda k, shape: jax.random.normal(k, shape, f32)
    return {
        'x': nrm(ks[0], (BATCH, SEQ, D_MODEL)),
        'norm1_g': 1.0 + 0.02 * nrm(ks[1], (DEPTH, D_MODEL)),
        'w_in': nrm(ks[2], (DEPTH, D_MODEL, IN_TOTAL)) * D_MODEL ** -0.5,
        'attn_norm_g': 1.0 + 0.02 * nrm(ks[3], (DEPTH, ATTN_WIDTH)),
        'hgrn_norm_g': 1.0 + 0.02 * nrm(ks[4], (DEPTH, HGRN_WIDTH)),
        'hgrn_lb_logits': 0.1 * nrm(ks[5], (DEPTH + 1, HGRN_WIDTH)),
        'w_out': nrm(ks[6], (DEPTH, MIX_WIDTH, D_MODEL)) * MIX_WIDTH ** -0.5,
        'norm2_g': 1.0 + 0.02 * nrm(ks[7], (DEPTH, D_MODEL)),
        'w_up': nrm(ks[8], (DEPTH, D_MODEL, 2 * D_FF)) * D_MODEL ** -0.5,
        'conv_w': nrm(ks[9], (DEPTH, CONV_WIDTH, D_FF)) * CONV_WIDTH ** -0.5,
        'conv_b': 0.02 * nrm(ks[10], (DEPTH, D_FF)),
        'w_down': nrm(ks[11], (DEPTH, D_FF, D_MODEL)) * D_FF ** -0.5,
        'final_norm_g': 1.0 + 0.02 * nrm(ks[12], (D_MODEL,)),
    }


def reference(x, norm1_g, w_in, attn_norm_g, hgrn_norm_g, hgrn_lb_logits, w_out,
              norm2_g, w_up, conv_w, conv_b, w_down, final_norm_g):
    B, S, _ = x.shape
    lower_bounds = jnp.cumsum(jax.nn.softmax(hgrn_lb_logits.astype(jnp.float32), axis=0), axis=0)
    h = x
    for layer in range(DEPTH):
        u = rms_norm(h, norm1_g[layer])
        proj = u @ w_in[layer]
        aq, ak, av, hq, hf, hi, hg = jnp.split(proj, IN_SPLITS, axis=-1)
        attn = dilated_attention(aq.reshape(B, S, ATTN_HEADS, ATTN_HEAD_DIM),
                                 ak.reshape(B, S, ATTN_HEADS, ATTN_HEAD_DIM),
                                 av.reshape(B, S, ATTN_HEADS, ATTN_HEAD_DIM))
        attn = rms_norm(attn.reshape(B, S, ATTN_WIDTH), attn_norm_g[layer])
        lb = lower_bounds[layer].reshape(HGRN_HEADS, HGRN_FDIM)
        rec = hgrn2_mixer(hq.reshape(B, S, HGRN_HEADS, HGRN_FDIM),
                          hf.reshape(B, S, HGRN_HEADS, HGRN_FDIM),
                          hi.reshape(B, S, HGRN_HEADS, HGRN_IDIM), lb)
        rec = rms_norm(rec, hgrn_norm_g[layer].reshape(HGRN_HEADS, HGRN_IDIM))
        rec = rec * jax.nn.silu(hg.astype(jnp.float32).reshape(B, S, HGRN_HEADS, HGRN_IDIM))
        mixed = jnp.concatenate([attn.astype(jnp.float32), rec.reshape(B, S, HGRN_WIDTH)], axis=-1)
        h = h + mixed.astype(h.dtype) @ w_out[layer]
        u = rms_norm(h, norm2_g[layer])
        h = h + conv_glu(u, w_up[layer], conv_w[layer], conv_b[layer], w_down[layer]).astype(h.dtype)
    return rms_norm(h, final_norm_g)
```

```python
import functools
import math

import jax
import jax.numpy as jnp
from jax import lax
from jax.experimental import pallas as pl
from jax.experimental.pallas import tpu as pltpu

F32 = jnp.float32
BF16 = jnp.bfloat16

D_MODEL = 1024
ATTN_WIDTH = 512
ATTN_HEADS = 8
ATTN_HEAD_DIM = 64
ATTN_BLOCK = 128
DILATIONS = (1, 4, 16)
HGRN_WIDTH = 512
HGRN_HEADS = 4
HGRN_DIM = 128
HGRN_CHUNK = 64
D_FF = 2816
CONV_WIDTH = 3
NORM_EPS = 1e-6
QKV_WIDTH = 3 * ATTN_WIDTH
HG_WIDTH = 4 * HGRN_WIDTH

LANES = 128
SUBLANES = 8
FF_CHUNK = 256
N_FF_CHUNKS = D_FF // FF_CHUNK
HGRN_ROWS = 256
VMEM_LIMIT = 56 * 1024 * 1024


def _rms(x, g):
    return x * lax.rsqrt(jnp.mean(x * x, axis=-1, keepdims=True) + NORM_EPS) * g


def _inproj_kernel(x_ref, g_ref, w_ref, qkv_ref, hg_ref):
    u = _rms(x_ref[...], g_ref[...]).astype(BF16)
    qkv_ref[...] = jnp.dot(u, w_ref[:, :QKV_WIDTH], preferred_element_type=F32)
    hg_ref[...] = jnp.dot(u, w_ref[:, QKV_WIDTH:],
                          preferred_element_type=F32).astype(hg_ref.dtype)


def _inproj(x2, g, w, tm):
    T = x2.shape[0]
    return pl.pallas_call(
        _inproj_kernel,
        grid=(T // tm,),
        in_specs=[
            pl.BlockSpec((tm, D_MODEL), lambda i: (i, 0)),
            pl.BlockSpec((1, D_MODEL), lambda i: (0, 0)),
            pl.BlockSpec((D_MODEL, QKV_WIDTH + HG_WIDTH), lambda i: (0, 0),
                         pipeline_mode=pl.Buffered(1)),
        ],
        out_specs=[
            pl.BlockSpec((tm, QKV_WIDTH), lambda i: (i, 0)),
            pl.BlockSpec((tm, HG_WIDTH), lambda i: (i, 0)),
        ],
        out_shape=[
            jax.ShapeDtypeStruct((T, QKV_WIDTH), F32),
            jax.ShapeDtypeStruct((T, HG_WIDTH), BF16),
        ],
        compiler_params=pltpu.CompilerParams(
            dimension_semantics=("arbitrary",), vmem_limit_bytes=VMEM_LIMIT),
        name="inproj",
    )(x2, g, w)


def _attn_bias():
    blk = ATTN_BLOCK
    qi = jnp.arange(2 * blk)[:, None] % blk
    kj = jnp.arange(2 * blk)[None, :]
    generic = jnp.where(kj < blk, kj >= qi, (kj - blk) <= qi)
    first = (kj < blk) & (kj <= qi)
    valid = jnp.stack([generic, first])
    return jnp.where(valid, 0.0, -jnp.inf).astype(F32)


def _attn_kernel(q_ref, k_ref, v_ref, bias_ref, o_ref, m_sc, l_sc, acc_sc, *, seq):
    blk = ATTN_BLOCK
    n_blocks = seq // blk
    head0 = lax.broadcasted_iota(jnp.int32, (blk, LANES), 1) < ATTN_HEAD_DIM
    scale = ATTN_HEAD_DIM ** -0.5

    def block_softmax(qrows, krows, first):
        q2 = q_ref[qrows, :] * scale
        zero = jnp.zeros_like(q2)
        qs = jnp.concatenate([jnp.where(head0, q2, zero),
                              jnp.where(head0, zero, q2)], axis=0).astype(BF16)
        kw = k_ref[krows, :].astype(BF16)
        vw = v_ref[krows, :].astype(BF16)
        s = lax.dot_general(qs, kw, (((1,), (1,)), ((), ())),
                            preferred_element_type=F32)
        s = s + bias_ref[first]
        m = jnp.max(s, axis=-1, keepdims=True)
        p = jnp.exp(s - m)
        l = jnp.sum(p, axis=-1, keepdims=True)
        pv = jnp.dot(p.astype(BF16), vw, preferred_element_type=F32)
        o_b = jnp.where(head0, pv[:blk], pv[blk:])
        m_b = jnp.where(head0, jnp.broadcast_to(m[:blk], (blk, LANES)),
                        jnp.broadcast_to(m[blk:], (blk, LANES)))
        l_b = jnp.where(head0, jnp.broadcast_to(l[:blk], (blk, LANES)),
                        jnp.broadcast_to(l[blk:], (blk, LANES)))
        return o_b, m_b, l_b

    def dense_body(n, carry):
        first = (n == 0).astype(jnp.int32)
        qstart = pl.multiple_of(n * blk, blk)
        kstart = pl.multiple_of(jnp.maximum(n - 1, 0) * blk, blk)
        qrows = pl.ds(qstart, blk)
        o_b, m_b, l_b = block_softmax(qrows, pl.ds(kstart, 2 * blk), first)
        acc_sc[qrows, :] = o_b
        m_sc[qrows, :] = m_b
        l_sc[qrows, :] = l_b
        return carry

    lax.fori_loop(0, n_blocks, dense_body, 0)

    for d in DILATIONS[1:]:
        nb = n_blocks // d

        def dil_body(j, carry, d=d, nb=nb):
            r = j // nb
            n = j % nb
            first = (n == 0).astype(jnp.int32)
            qrows = pl.ds(n * (blk * d) + r, blk, stride=d)
            krows = pl.ds(jnp.maximum(n - 1, 0) * (blk * d) + r, 2 * blk, stride=d)
            o_b, m_b, l_b = block_softmax(qrows, krows, first)
            m_old = m_sc[qrows, :]
            m_new = jnp.maximum(m_old, m_b)
            a_old = jnp.exp(m_old - m_new)
            a_b = jnp.exp(m_b - m_new)
            l_sc[qrows, :] = a_old * l_sc[qrows, :] + a_b * l_b
            acc_sc[qrows, :] = a_old * acc_sc[qrows, :] + a_b * o_b
            m_sc[qrows, :] = m_new
            return carry

        lax.fori_loop(0, n_blocks, dil_body, 0)

    def out_body(n, carry):
        rows = pl.ds(pl.multiple_of(n * blk, blk), blk)
        o_ref[rows, :] = acc_sc[rows, :] / l_sc[rows, :]
        return carry

    lax.fori_loop(0, n_blocks, out_body, 0)


def _attention(qkv, bias):
    B, S, _ = qkv.shape
    n_pairs = ATTN_HEADS // 2
    blockspec = lambda off: pl.BlockSpec((None, S, LANES), lambda b, p: (b, 0, p + off))
    return pl.pallas_call(
        functools.partial(_attn_kernel, seq=S),
        grid=(B, n_pairs),
        in_specs=[
            blockspec(0), blockspec(n_pairs), blockspec(2 * n_pairs),
            pl.BlockSpec((2, 2 * ATTN_BLOCK, 2 * ATTN_BLOCK), lambda b, p: (0, 0, 0)),
        ],
        out_specs=pl.BlockSpec((None, S, LANES), lambda b, p: (b, 0, p)),
        out_shape=jax.ShapeDtypeStruct((B, S, ATTN_WIDTH), F32),
        scratch_shapes=[pltpu.VMEM((S, LANES), F32)] * 3,
        compiler_params=pltpu.CompilerParams(
            dimension_semantics=("arbitrary", "arbitrary"),
            vmem_limit_bytes=VMEM_LIMIT),
        name="dilated_attn",
    )(qkv, qkv, qkv, bias)


def _chunk_cumsum(x, row_in_chunk):
    shift = 1
    while shift < HGRN_CHUNK:
        moved = pltpu.roll(x, shift, axis=0)
        x = x + jnp.where(row_in_chunk >= shift, moved, 0.0)
        shift *= 2
    return x


def _hgrn_kernel(q_ref, f_ref, i_ref, g_ref, lbl_ref, gn_ref, o_ref, st_sc, *, seq):
    R = HGRN_ROWS
    C = HGRN_CHUNK
    logits = lbl_ref[...]
    e = jnp.exp(logits - jnp.max(logits, axis=0, keepdims=True))
    lb = e[0:1] / jnp.sum(e, axis=0, keepdims=True)
    gn = gn_ref[...]

    st_sc[...] = jnp.zeros_like(st_sc)
    row = lax.broadcasted_iota(jnp.int32, (R, R), 0)
    col = lax.broadcasted_iota(jnp.int32, (R, R), 1)
    causal = (row >= col) & ((row // C) == (col // C))
    row_in_chunk = lax.broadcasted_iota(jnp.int32, (R, HGRN_DIM), 0) % C

    def body(c, carry):
        rows = pl.ds(pl.multiple_of(c * R, R), R)
        q = q_ref[rows, :].astype(F32)
        f = f_ref[rows, :].astype(F32)
        iv = i_ref[rows, :].astype(BF16)
        g = g_ref[rows, :].astype(F32)
        qf = q * jax.nn.sigmoid(q)
        forget = lb + (1.0 - lb) * jax.nn.sigmoid(f)
        key = 1.0 - forget
        b = _chunk_cumsum(jnp.log(forget), row_in_chunk)
        b_end = jnp.concatenate(
            [jnp.broadcast_to(b[j * C + C - 1:j * C + C], (C, HGRN_DIM))
             for j in range(R // C)], axis=0)
        q_dec = (qf * jnp.exp(b)).astype(BF16)
        k_inv = (key * jnp.exp(-b)).astype(BF16)
        k_end = (key * jnp.exp(b_end - b)).astype(BF16)
        a = lax.dot_general(q_dec, k_inv, (((1,), (1,)), ((), ())),
                            preferred_element_type=F32)
        a = jnp.where(causal, a, 0.0).astype(BF16)
        o_intra = jnp.dot(a, iv, preferred_element_type=F32)
        outs = []
        for j in range(R // C):
            sl = slice(j * C, (j + 1) * C)
            st = st_sc[...]
            o_inter = lax.dot_general(q_dec[sl], st.astype(BF16),
                                      (((1,), (1,)), ((), ())),
                                      preferred_element_type=F32)
            u_t = lax.dot_general(iv[sl], k_end[sl], (((0,), (0,)), ((), ())),
                                  preferred_element_type=F32)
            decay = jnp.exp(b_end[j * C:j * C + 1])
            st_sc[...] = st * decay + u_t
            outs.append(o_intra[sl] + o_inter)
        o = jnp.concatenate(outs, axis=0)
        o = _rms(o, gn)
        o_ref[rows, :] = (o * (g * jax.nn.sigmoid(g))).astype(o_ref.dtype)
        return carry

    lax.fori_loop(0, seq // R, body, 0)


def _hgrn(hg, lb_logits, gn):
    B, S, _ = hg.shape
    H = HGRN_HEADS
    n_rows = lb_logits.shape[1]
    blockspec = lambda off: pl.BlockSpec((None, S, HGRN_DIM), lambda b, h: (b, 0, h + off))
    return pl.pallas_call(
        functools.partial(_hgrn_kernel, seq=S),
        grid=(B, H),
        in_specs=[
            blockspec(0), blockspec(H), blockspec(2 * H), blockspec(3 * H),
            pl.BlockSpec((None, n_rows, HGRN_DIM), lambda b, h: (h, 0, 0)),
            pl.BlockSpec((None, 1, HGRN_DIM), lambda b, h: (h, 0, 0)),
        ],
        out_specs=pl.BlockSpec((None, S, HGRN_DIM), lambda b, h: (b, 0, h)),
        out_shape=jax.ShapeDtypeStruct((B, S, HGRN_WIDTH), BF16),
        scratch_shapes=[pltpu.VMEM((HGRN_DIM, HGRN_DIM), F32)],
        compiler_params=pltpu.CompilerParams(
            dimension_semantics=("arbitrary", "arbitrary"),
            vmem_limit_bytes=VMEM_LIMIT),
        name="hgrn2",
    )(hg, hg, hg, hg, lb_logits, gn)


def _ffn_kernel(x_ref, attn_ref, rec_ref, ga_ref, wout_ref, g2_ref, wup_ref,
                cw_ref, cb_ref, wdown_ref, gf_ref, o_ref,
                h_sc, u_sc, acc_sc, carry_sc, *, tiles_per_seq):
    tm = x_ref.shape[0]
    i = pl.program_id(0)

    @pl.when(i % tiles_per_seq == 0)
    def _():
        carry_sc[...] = jnp.zeros_like(carry_sc)

    an = _rms(attn_ref[...], ga_ref[...]).astype(BF16)
    mix = jnp.dot(an, wout_ref[:ATTN_WIDTH, :], preferred_element_type=F32)
    mix = mix + jnp.dot(rec_ref[...], wout_ref[ATTN_WIDTH:, :],
                        preferred_element_type=F32)
    h = x_ref[...] + mix
    h_sc[...] = h
    u_sc[...] = _rms(h, g2_ref[...]).astype(BF16)
    acc_sc[...] = jnp.zeros_like(acc_sc)

    row = lax.broadcasted_iota(jnp.int32, (tm, FF_CHUNK), 0)
    sqrt_half = math.sqrt(0.5)

    def chunk(c, carry):
        u = u_sc[...]
        gate = jnp.dot(u, wup_ref[0, c], preferred_element_type=F32)
        val = jnp.dot(u, wup_ref[1, c], preferred_element_type=F32)
        prev = carry_sc[c]
        carry_sc[c] = gate[tm - SUBLANES:, :]
        p1 = jnp.broadcast_to(prev[SUBLANES - 1:SUBLANES], (tm, FF_CHUNK))
        p2 = jnp.broadcast_to(prev[SUBLANES - 2:SUBLANES - 1], (tm, FF_CHUNK))
        g1 = jnp.where(row == 0, p1, pltpu.roll(gate, 1, axis=0))
        g2 = jnp.where(row == 0, p2, jnp.where(row == 1, p1,
                                               pltpu.roll(gate, 2, axis=0)))
        cw = cw_ref[c]
        conv = cb_ref[c] + cw[0:1] * g2 + cw[1:2] * g1 + cw[2:3] * gate
        act = 0.5 * conv * (1.0 + lax.erf(conv * sqrt_half)) * val
        acc_sc[...] += jnp.dot(act.astype(BF16), wdown_ref[c],
                               preferred_element_type=F32)
        return carry

    lax.fori_loop(0, N_FF_CHUNKS, chunk, 0)
    o_ref[...] = _rms(h_sc[...] + acc_sc[...], gf_ref[...])


def _ffn(x2, attn2, rec2, ga, wout, g2, wup, cw, cb, wdown, gf, tm, seq):
    T = x2.shape[0]
    const = lambda shape: pl.BlockSpec(shape, lambda i: (0,) * len(shape),
                                       pipeline_mode=pl.Buffered(1))
    return pl.pallas_call(
        functools.partial(_ffn_kernel, tiles_per_seq=seq // tm),
        grid=(T // tm,),
        in_specs=[
            pl.BlockSpec((tm, D_MODEL), lambda i: (i, 0)),
            pl.BlockSpec((tm, ATTN_WIDTH), lambda i: (i, 0)),
            pl.BlockSpec((tm, HGRN_WIDTH), lambda i: (i, 0)),
            const((1, ATTN_WIDTH)),
            const((D_MODEL, D_MODEL)),
            const((1, D_MODEL)),
            const((2, N_FF_CHUNKS, D_MODEL, FF_CHUNK)),
            const((N_FF_CHUNKS, CONV_WIDTH, FF_CHUNK)),
            const((N_FF_CHUNKS, 1, FF_CHUNK)),
            const((N_FF_CHUNKS, FF_CHUNK, D_MODEL)),
            const((1, D_MODEL)),
        ],
        out_specs=pl.BlockSpec((tm, D_MODEL), lambda i: (i, 0)),
        out_shape=jax.ShapeDtypeStruct((T, D_MODEL), F32),
        scratch_shapes=[
            pltpu.VMEM((tm, D_MODEL), F32),
            pltpu.VMEM((tm, D_MODEL), BF16),
            pltpu.VMEM((tm, D_MODEL), F32),
            pltpu.VMEM((N_FF_CHUNKS, SUBLANES, FF_CHUNK), F32),
        ],
        compiler_params=pltpu.CompilerParams(
            dimension_semantics=("arbitrary",), vmem_limit_bytes=VMEM_LIMIT),
        name="outproj_convglu",
    )(x2, attn2, rec2, ga, wout, g2, wup, cw, cb, wdown, gf)


def kernel(x, norm1_g, w_in, attn_norm_g, hgrn_norm_g, hgrn_lb_logits, w_out,
           norm2_g, w_up, conv_w, conv_b, w_down, final_norm_g):
    B, S, D = x.shape
    depth = w_in.shape[0]
    assert depth == 1 and D == D_MODEL
    assert S % (ATTN_BLOCK * DILATIONS[-1]) == 0 and S % HGRN_ROWS == 0
    tm = 512
    assert S % tm == 0
    T = B * S
    layer = 0
    x2 = x.reshape(T, D)

    qkv, hg = _inproj(x2, norm1_g[layer].reshape(1, D), w_in[layer].astype(BF16), tm)

    attn = _attention(qkv.reshape(B, S, QKV_WIDTH), _attn_bias())

    lbl = hgrn_lb_logits.astype(F32).reshape(depth + 1, HGRN_HEADS, HGRN_DIM)
    lbl = lbl.transpose(1, 0, 2)
    gn = hgrn_norm_g[layer].reshape(HGRN_HEADS, 1, HGRN_DIM)
    rec = _hgrn(hg.reshape(B, S, HG_WIDTH), lbl, gn)

    wup = w_up[layer].astype(BF16).reshape(D, 2, N_FF_CHUNKS, FF_CHUNK).transpose(1, 2, 0, 3)
    cw = conv_w[layer].reshape(CONV_WIDTH, N_FF_CHUNKS, FF_CHUNK).transpose(1, 0, 2)
    cb = conv_b[layer].reshape(N_FF_CHUNKS, 1, FF_CHUNK)
    wdown = w_down[layer].astype(BF16).reshape(N_FF_CHUNKS, FF_CHUNK, D)
    out = _ffn(x2, attn.reshape(T, ATTN_WIDTH), rec.reshape(T, HGRN_WIDTH),
               attn_norm_g[layer].reshape(1, ATTN_WIDTH), w_out[layer].astype(BF16),
               norm2_g[layer].reshape(1, D), wup, cw, cb, wdown,
               final_norm_g.reshape(1, D), tm, S)
    return out.reshape(B, S, D)
```

```python
import functools
import math

import jax
import jax.numpy as jnp
from jax import lax
from jax.experimental import pallas as pl
from jax.experimental.pallas import tpu as pltpu

F32 = jnp.float32
BF16 = jnp.bfloat16

D_MODEL = 1024
ATTN_WIDTH = 512
ATTN_HEADS = 8
ATTN_HEAD_DIM = 64
ATTN_BLOCK = 128
DILATIONS = (1, 4, 16)
HGRN_WIDTH = 512
HGRN_HEADS = 4
HGRN_DIM = 128
HGRN_CHUNK = 64
D_FF = 2816
CONV_WIDTH = 3
NORM_EPS = 1e-6
QKV_WIDTH = 3 * ATTN_WIDTH
HG_WIDTH = 4 * HGRN_WIDTH

LANES = 128
SUBLANES = 8
FF_CHUNK = 256
N_FF_CHUNKS = D_FF // FF_CHUNK
HGRN_ROWS = 256
ATTN_UNROLL = 16
VMEM_LIMIT = 56 * 1024 * 1024


def _rms(x, g):
    return x * lax.rsqrt(jnp.mean(x * x, axis=-1, keepdims=True) + NORM_EPS) * g


def _inproj_kernel(x_ref, g_ref, w_ref, qkv_ref, hg_ref):
    u = _rms(x_ref[...], g_ref[...]).astype(BF16)
    qkv_ref[...] = jnp.dot(u, w_ref[:, :QKV_WIDTH], preferred_element_type=F32)
    hg_ref[...] = jnp.dot(u, w_ref[:, QKV_WIDTH:],
                          preferred_element_type=F32).astype(hg_ref.dtype)


def _inproj(x2, g, w, tm):
    T = x2.shape[0]
    return pl.pallas_call(
        _inproj_kernel,
        grid=(T // tm,),
        in_specs=[
            pl.BlockSpec((tm, D_MODEL), lambda i: (i, 0)),
            pl.BlockSpec((1, D_MODEL), lambda i: (0, 0)),
            pl.BlockSpec((D_MODEL, QKV_WIDTH + HG_WIDTH), lambda i: (0, 0),
                         pipeline_mode=pl.Buffered(1)),
        ],
        out_specs=[
            pl.BlockSpec((tm, QKV_WIDTH), lambda i: (i, 0)),
            pl.BlockSpec((tm, HG_WIDTH), lambda i: (i, 0)),
        ],
        out_shape=[
            jax.ShapeDtypeStruct((T, QKV_WIDTH), F32),
            jax.ShapeDtypeStruct((T, HG_WIDTH), BF16),
        ],
        compiler_params=pltpu.CompilerParams(
            dimension_semantics=("arbitrary",), vmem_limit_bytes=VMEM_LIMIT),
        name="inproj",
    )(x2, g, w)


def _attn_bias():
    blk = ATTN_BLOCK
    qi = jnp.arange(2 * blk)[:, None] % blk
    kj = jnp.arange(2 * blk)[None, :]
    generic = jnp.where(kj < blk, kj >= qi, (kj - blk) <= qi)
    first = (kj < blk) & (kj <= qi)
    valid = jnp.stack([generic, first])
    return jnp.where(valid, 0.0, -jnp.inf).astype(F32)


def _attn_kernel(q_ref, k_ref, v_ref, bias_ref, o_ref, m_sc, l_sc, acc_sc, *, seq):
    blk = ATTN_BLOCK
    n_blocks = seq // blk
    head0 = lax.broadcasted_iota(jnp.int32, (blk, LANES), 1) < ATTN_HEAD_DIM
    scale = ATTN_HEAD_DIM ** -0.5

    def block_softmax(qrows, krows, first):
        q2 = q_ref[qrows, :] * scale
        zero = jnp.zeros_like(q2)
        qs = jnp.concatenate([jnp.where(head0, q2, zero),
                              jnp.where(head0, zero, q2)], axis=0).astype(BF16)
        kw = k_ref[krows, :].astype(BF16)
        vw = v_ref[krows, :].astype(BF16)
        s = lax.dot_general(qs, kw, (((1,), (1,)), ((), ())),
                            preferred_element_type=F32)
        s = s + bias_ref[first]
        m = jnp.max(s, axis=-1, keepdims=True)
        p = jnp.exp(s - m)
        l = jnp.sum(p, axis=-1, keepdims=True)
        pv = jnp.dot(p.astype(BF16), vw, preferred_element_type=F32)
        o_b = jnp.where(head0, pv[:blk], pv[blk:])
        m_b = jnp.where(head0, jnp.broadcast_to(m[:blk], (blk, LANES)),
                        jnp.broadcast_to(m[blk:], (blk, LANES)))
        l_b = jnp.where(head0, jnp.broadcast_to(l[:blk], (blk, LANES)),
                        jnp.broadcast_to(l[blk:], (blk, LANES)))
        return o_b, m_b, l_b

    for bi, d in enumerate(DILATIONS):
        nb = n_blocks // d

        def branch_body(it, carry, bi=bi, d=d, nb=nb):
            for g in range(ATTN_UNROLL):
                j = it * ATTN_UNROLL + g
                r = j // nb
                n = j % nb
                first = (n == 0).astype(jnp.int32)
                if d == 1:
                    qrows = pl.ds(pl.multiple_of(n * blk, blk), blk)
                    krows = pl.ds(pl.multiple_of(jnp.maximum(n - 1, 0) * blk, blk), 2 * blk)
                else:
                    qrows = pl.ds(n * (blk * d) + r, blk, stride=d)
                    krows = pl.ds(jnp.maximum(n - 1, 0) * (blk * d) + r, 2 * blk, stride=d)
                o_b, m_b, l_b = block_softmax(qrows, krows, first)
                acc_sc[bi, qrows, :] = o_b
                m_sc[bi, qrows, :] = m_b
                l_sc[bi, qrows, :] = l_b
            return carry

        lax.fori_loop(0, n_blocks // ATTN_UNROLL, branch_body, 0)

    def merge_body(n, carry):
        rows = pl.ds(pl.multiple_of(n * blk, blk), blk)
        ms = [m_sc[bi, rows, :] for bi in range(len(DILATIONS))]
        m_all = functools.reduce(jnp.maximum, ms)
        den = jnp.zeros((blk, LANES), F32)
        num = jnp.zeros((blk, LANES), F32)
        for bi in range(len(DILATIONS)):
            w = jnp.exp(ms[bi] - m_all)
            den = den + w * l_sc[bi, rows, :]
            num = num + w * acc_sc[bi, rows, :]
        o_ref[rows, :] = num / den
        return carry

    lax.fori_loop(0, n_blocks, merge_body, 0)


def _attention(qkv, bias):
    B, S, _ = qkv.shape
    n_pairs = ATTN_HEADS // 2
    blockspec = lambda off: pl.BlockSpec((None, S, LANES), lambda b, p: (b, 0, p + off))
    return pl.pallas_call(
        functools.partial(_attn_kernel, seq=S),
        grid=(B, n_pairs),
        in_specs=[
            blockspec(0), blockspec(n_pairs), blockspec(2 * n_pairs),
            pl.BlockSpec((2, 2 * ATTN_BLOCK, 2 * ATTN_BLOCK), lambda b, p: (0, 0, 0)),
        ],
        out_specs=pl.BlockSpec((None, S, LANES), lambda b, p: (b, 0, p)),
        out_shape=jax.ShapeDtypeStruct((B, S, ATTN_WIDTH), F32),
        scratch_shapes=[pltpu.VMEM((len(DILATIONS), S, LANES), F32)] * 3,
        compiler_params=pltpu.CompilerParams(
            dimension_semantics=("arbitrary", "arbitrary"),
            vmem_limit_bytes=VMEM_LIMIT),
        name="dilated_attn",
    )(qkv, qkv, qkv, bias)


def _chunk_cumsum(x, row_in_chunk):
    shift = 1
    while shift < HGRN_CHUNK:
        moved = pltpu.roll(x, shift, axis=0)
        x = x + jnp.where(row_in_chunk >= shift, moved, 0.0)
        shift *= 2
    return x


def _hgrn_kernel(q_ref, f_ref, i_ref, g_ref, lbl_ref, gn_ref, o_ref, st_sc, *, seq):
    R = HGRN_ROWS
    C = HGRN_CHUNK
    logits = lbl_ref[...]
    e = jnp.exp(logits - jnp.max(logits, axis=0, keepdims=True))
    lb = e[0:1] / jnp.sum(e, axis=0, keepdims=True)
    gn = gn_ref[...]

    st_sc[...] = jnp.zeros_like(st_sc)
    row = lax.broadcasted_iota(jnp.int32, (R, R), 0)
    col = lax.broadcasted_iota(jnp.int32, (R, R), 1)
    causal = (row >= col) & ((row // C) == (col // C))
    row_in_chunk = lax.broadcasted_iota(jnp.int32, (R, HGRN_DIM), 0) % C

    def body(c, carry):
        rows = pl.ds(pl.multiple_of(c * R, R), R)
        q = q_ref[rows, :].astype(F32)
        f = f_ref[rows, :].astype(F32)
        iv = i_ref[rows, :].astype(BF16)
        g = g_ref[rows, :].astype(F32)
        qf = q * jax.nn.sigmoid(q)
        forget = lb + (1.0 - lb) * jax.nn.sigmoid(f)
        key = 1.0 - forget
        b = _chunk_cumsum(jnp.log(forget), row_in_chunk)
        b_end = jnp.concatenate(
            [jnp.broadcast_to(b[j * C + C - 1:j * C + C], (C, HGRN_DIM))
             for j in range(R // C)], axis=0)
        q_dec = (qf * jnp.exp(b)).astype(BF16)
        k_inv = (key * jnp.exp(-b)).astype(BF16)
        k_end = (key * jnp.exp(b_end - b)).astype(BF16)
        a = lax.dot_general(q_dec, k_inv, (((1,), (1,)), ((), ())),
                            preferred_element_type=F32)
        a = jnp.where(causal, a, 0.0).astype(BF16)
        o_intra = jnp.dot(a, iv, preferred_element_type=F32)
        outs = []
        for j in range(R // C):
            sl = slice(j * C, (j + 1) * C)
            st = st_sc[...]
            o_inter = lax.dot_general(q_dec[sl], st.astype(BF16),
                                      (((1,), (1,)), ((), ())),
                                      preferred_element_type=F32)
            u_t = lax.dot_general(iv[sl], k_end[sl], (((0,), (0,)), ((), ())),
                                  preferred_element_type=F32)
            decay = jnp.exp(b_end[j * C:j * C + 1])
            st_sc[...] = st * decay + u_t
            outs.append(o_intra[sl] + o_inter)
        o = jnp.concatenate(outs, axis=0)
        o = _rms(o, gn)
        o_ref[rows, :] = (o * (g * jax.nn.sigmoid(g))).astype(o_ref.dtype)
        return carry

    lax.fori_loop(0, seq // R, body, 0)


def _hgrn(hg, lb_logits, gn):
    B, S, _ = hg.shape
    H = HGRN_HEADS
    n_rows = lb_logits.shape[1]
    blockspec = lambda off: pl.BlockSpec((None, S, HGRN_DIM), lambda b, h: (b, 0, h + off))
    return pl.pallas_call(
        functools.partial(_hgrn_kernel, seq=S),
        grid=(B, H),
        in_specs=[
            blockspec(0), blockspec(H), blockspec(2 * H), blockspec(3 * H),
            pl.BlockSpec((None, n_rows, HGRN_DIM), lambda b, h: (h, 0, 0)),
            pl.BlockSpec((None, 1, HGRN_DIM), lambda b, h: (h, 0, 0)),
        ],
        out_specs=pl.BlockSpec((None, S, HGRN_DIM), lambda b, h: (b, 0, h)),
        out_shape=jax.ShapeDtypeStruct((B, S, HGRN_WIDTH), BF16),
        scratch_shapes=[pltpu.VMEM((HGRN_DIM, HGRN_DIM), F32)],
        compiler_params=pltpu.CompilerParams(
            dimension_semantics=("arbitrary", "arbitrary"),
            vmem_limit_bytes=VMEM_LIMIT),
        name="hgrn2",
    )(hg, hg, hg, hg, lb_logits, gn)


def _ffn_kernel(x_ref, attn_ref, rec_ref, ga_ref, wout_ref, g2_ref, wup_ref,
                cw_ref, cb_ref, wdown_ref, gf_ref, o_ref,
                h_sc, u_sc, acc_sc, carry_sc, *, tiles_per_seq):
    tm = x_ref.shape[0]
    i = pl.program_id(0)

    @pl.when(i % tiles_per_seq == 0)
    def _():
        carry_sc[...] = jnp.zeros_like(carry_sc)

    an = _rms(attn_ref[...], ga_ref[...]).astype(BF16)
    mix = jnp.dot(an, wout_ref[:ATTN_WIDTH, :], preferred_element_type=F32)
    mix = mix + jnp.dot(rec_ref[...], wout_ref[ATTN_WIDTH:, :],
                        preferred_element_type=F32)
    h = x_ref[...] + mix
    h_sc[...] = h
    u_sc[...] = _rms(h, g2_ref[...]).astype(BF16)
    acc_sc[...] = jnp.zeros_like(acc_sc)

    row = lax.broadcasted_iota(jnp.int32, (tm, FF_CHUNK), 0)
    sqrt_half = math.sqrt(0.5)

    def chunk(c, carry):
        u = u_sc[...]
        gate = jnp.dot(u, wup_ref[0, c], preferred_element_type=F32)
        val = jnp.dot(u, wup_ref[1, c], preferred_element_type=F32)
        prev = carry_sc[c]
        carry_sc[c] = gate[tm - SUBLANES:, :]
        p1 = jnp.broadcast_to(prev[SUBLANES - 1:SUBLANES], (tm, FF_CHUNK))
        p2 = jnp.broadcast_to(prev[SUBLANES - 2:SUBLANES - 1], (tm, FF_CHUNK))
        g1 = jnp.where(row == 0, p1, pltpu.roll(gate, 1, axis=0))
        g2 = jnp.where(row == 0, p2, jnp.where(row == 1, p1,
                                               pltpu.roll(gate, 2, axis=0)))
        cw = cw_ref[c]
        conv = cb_ref[c] + cw[0:1] * g2 + cw[1:2] * g1 + cw[2:3] * gate
        act = 0.5 * conv * (1.0 + lax.erf(conv * sqrt_half)) * val
        acc_sc[...] += jnp.dot(act.astype(BF16), wdown_ref[c],
                               preferred_element_type=F32)
        return carry

    lax.fori_loop(0, N_FF_CHUNKS, chunk, 0)
    o_ref[...] = _rms(h_sc[...] + acc_sc[...], gf_ref[...])


def _ffn(x2, attn2, rec2, ga, wout, g2, wup, cw, cb, wdown, gf, tm, seq):
    T = x2.shape[0]
    const = lambda shape: pl.BlockSpec(shape, lambda i: (0,) * len(shape),
                                       pipeline_mode=pl.Buffered(1))
    return pl.pallas_call(
        functools.partial(_ffn_kernel, tiles_per_seq=seq // tm),
        grid=(T // tm,),
        in_specs=[
            pl.BlockSpec((tm, D_MODEL), lambda i: (i, 0)),
            pl.BlockSpec((tm, ATTN_WIDTH), lambda i: (i, 0)),
            pl.BlockSpec((tm, HGRN_WIDTH), lambda i: (i, 0)),
            const((1, ATTN_WIDTH)),
            const((D_MODEL, D_MODEL)),
            const((1, D_MODEL)),
            const((2, N_FF_CHUNKS, D_MODEL, FF_CHUNK)),
            const((N_FF_CHUNKS, CONV_WIDTH, FF_CHUNK)),
            const((N_FF_CHUNKS, 1, FF_CHUNK)),
            const((N_FF_CHUNKS, FF_CHUNK, D_MODEL)),
            const((1, D_MODEL)),
        ],
        out_specs=pl.BlockSpec((tm, D_MODEL), lambda i: (i, 0)),
        out_shape=jax.ShapeDtypeStruct((T, D_MODEL), F32),
        scratch_shapes=[
            pltpu.VMEM((tm, D_MODEL), F32),
            pltpu.VMEM((tm, D_MODEL), BF16),
            pltpu.VMEM((tm, D_MODEL), F32),
            pltpu.VMEM((N_FF_CHUNKS, SUBLANES, FF_CHUNK), F32),
        ],
        compiler_params=pltpu.CompilerParams(
            dimension_semantics=("arbitrary",), vmem_limit_bytes=VMEM_LIMIT),
        name="outproj_convglu",
    )(x2, attn2, rec2, ga, wout, g2, wup, cw, cb, wdown, gf)


def kernel(x, norm1_g, w_in, attn_norm_g, hgrn_norm_g, hgrn_lb_logits, w_out,
           norm2_g, w_up, conv_w, conv_b, w_down, final_norm_g):
    B, S, D = x.shape
    depth = w_in.shape[0]
    assert depth == 1 and D == D_MODEL
    assert S % (ATTN_BLOCK * DILATIONS[-1]) == 0 and S % HGRN_ROWS == 0
    tm = 512
    assert S % tm == 0
    T = B * S
    layer = 0
    x2 = x.reshape(T, D)

    qkv, hg = _inproj(x2, norm1_g[layer].reshape(1, D), w_in[layer].astype(BF16), tm)

    attn = _attention(qkv.reshape(B, S, QKV_WIDTH), _attn_bias())

    lbl = hgrn_lb_logits.astype(F32).reshape(depth + 1, HGRN_HEADS, HGRN_DIM)
    lbl = lbl.transpose(1, 0, 2)
    gn = hgrn_norm_g[layer].reshape(HGRN_HEADS, 1, HGRN_DIM)
    rec = _hgrn(hg.reshape(B, S, HG_WIDTH), lbl, gn)

    wup = w_up[layer].astype(BF16).reshape(D, 2, N_FF_CHUNKS, FF_CHUNK).transpose(1, 2, 0, 3)
    cw = conv_w[layer].reshape(CONV_WIDTH, N_FF_CHUNKS, FF_CHUNK).transpose(1, 0, 2)
    cb = conv_b[layer].reshape(N_FF_CHUNKS, 1, FF_CHUNK)
    wdown = w_down[layer].astype(BF16).reshape(N_FF_CHUNKS, FF_CHUNK, D)
    out = _ffn(x2, attn.reshape(T, ATTN_WIDTH), rec.reshape(T, HGRN_WIDTH),
               attn_norm_g[layer].reshape(1, ATTN_WIDTH), w_out[layer].astype(BF16),
               norm2_g[layer].reshape(1, D), wup, cw, cb, wdown,
               final_norm_g.reshape(1, D), tm, S)
    return out.reshape(B, S, D)
```

```python
import functools
import math

import jax
import jax.numpy as jnp
from jax import lax
from jax.experimental import pallas as pl
from jax.experimental.pallas import tpu as pltpu

F32 = jnp.float32
BF16 = jnp.bfloat16

D_MODEL = 1024
ATTN_WIDTH = 512
ATTN_HEADS = 8
ATTN_HEAD_DIM = 64
ATTN_BLOCK = 128
DILATIONS = (1, 4, 16)
HGRN_WIDTH = 512
HGRN_HEADS = 4
HGRN_DIM = 128
HGRN_CHUNK = 64
D_FF = 2816
CONV_WIDTH = 3
NORM_EPS = 1e-6
QKV_WIDTH = 3 * ATTN_WIDTH
HG_WIDTH = 4 * HGRN_WIDTH

LANES = 128
SUBLANES = 8
FF_CHUNK = 256
N_FF_CHUNKS = D_FF // FF_CHUNK
HGRN_ROWS = 256
ATTN_UNROLL = 16
VMEM_LIMIT = 56 * 1024 * 1024


def _rms(x, g):
    return x * lax.rsqrt(jnp.mean(x * x, axis=-1, keepdims=True) + NORM_EPS) * g


def _inproj_kernel(x_ref, g_ref, w_ref, qkv_ref, hg_ref):
    u = _rms(x_ref[...], g_ref[...]).astype(BF16)
    qkv_ref[...] = jnp.dot(u, w_ref[:, :QKV_WIDTH], preferred_element_type=F32)
    hg_ref[...] = jnp.dot(u, w_ref[:, QKV_WIDTH:],
                          preferred_element_type=F32).astype(hg_ref.dtype)


def _inproj(x2, g, w, tm):
    T = x2.shape[0]
    return pl.pallas_call(
        _inproj_kernel,
        grid=(T // tm,),
        in_specs=[
            pl.BlockSpec((tm, D_MODEL), lambda i: (i, 0)),
            pl.BlockSpec((1, D_MODEL), lambda i: (0, 0)),
            pl.BlockSpec((D_MODEL, QKV_WIDTH + HG_WIDTH), lambda i: (0, 0),
                         pipeline_mode=pl.Buffered(1)),
        ],
        out_specs=[
            pl.BlockSpec((tm, QKV_WIDTH), lambda i: (i, 0)),
            pl.BlockSpec((tm, HG_WIDTH), lambda i: (i, 0)),
        ],
        out_shape=[
            jax.ShapeDtypeStruct((T, QKV_WIDTH), F32),
            jax.ShapeDtypeStruct((T, HG_WIDTH), BF16),
        ],
        compiler_params=pltpu.CompilerParams(
            dimension_semantics=("arbitrary",), vmem_limit_bytes=VMEM_LIMIT),
        name="inproj",
    )(x2, g, w)


def _attn_bias():
    blk = ATTN_BLOCK
    qi = jnp.arange(2 * blk)[:, None] % blk
    kj = jnp.arange(2 * blk)[None, :]
    generic = jnp.where(kj < blk, kj >= qi, (kj - blk) <= qi)
    first = (kj < blk) & (kj <= qi)
    valid = jnp.stack([generic, first])
    return jnp.where(valid, 0.0, -jnp.inf).astype(F32)


def _attn_kernel(q_ref, k_ref, v_ref, bias_ref, o_ref, m_sc, l_sc, acc_sc, *, seq):
    blk = ATTN_BLOCK
    n_blocks = seq // blk
    head0 = lax.broadcasted_iota(jnp.int32, (blk, LANES), 1) < ATTN_HEAD_DIM
    scale = ATTN_HEAD_DIM ** -0.5

    def block_softmax(qrows, krows, first):
        q2 = q_ref[qrows, :] * scale
        zero = jnp.zeros_like(q2)
        qs = jnp.concatenate([jnp.where(head0, q2, zero),
                              jnp.where(head0, zero, q2)], axis=0).astype(BF16)
        kw = k_ref[krows, :].astype(BF16)
        vw = v_ref[krows, :].astype(BF16)
        s = lax.dot_general(qs, kw, (((1,), (1,)), ((), ())),
                            preferred_element_type=F32)
        s = s + bias_ref[first]
        m = jnp.max(s, axis=-1, keepdims=True)
        p = jnp.exp(s - m)
        l = jnp.sum(p, axis=-1, keepdims=True)
        pv = jnp.dot(p.astype(BF16), vw, preferred_element_type=F32)
        o_b = jnp.where(head0, pv[:blk], pv[blk:])
        m_b = jnp.where(head0, jnp.broadcast_to(m[:blk], (blk, LANES)),
                        jnp.broadcast_to(m[blk:], (blk, LANES)))
        l_b = jnp.where(head0, jnp.broadcast_to(l[:blk], (blk, LANES)),
                        jnp.broadcast_to(l[blk:], (blk, LANES)))
        return o_b, m_b, l_b

    for bi, d in enumerate(DILATIONS):
        nb = n_blocks // d

        def branch_body(it, carry, bi=bi, d=d, nb=nb):
            for g in range(ATTN_UNROLL):
                j = it * ATTN_UNROLL + g
                r = j // nb
                n = j % nb
                first = jnp.where(n == 0, 1, 0)
                if d == 1:
                    qrows = pl.ds(pl.multiple_of(n * blk, blk), blk)
                    krows = pl.ds(pl.multiple_of(jnp.maximum(n - 1, 0) * blk, blk), 2 * blk)
                else:
                    qrows = pl.ds(n * (blk * d) + r, blk, stride=d)
                    krows = pl.ds(jnp.maximum(n - 1, 0) * (blk * d) + r, 2 * blk, stride=d)
                o_b, m_b, l_b = block_softmax(qrows, krows, first)
                acc_sc[bi, qrows, :] = o_b
                m_sc[bi, qrows, :] = m_b
                l_sc[bi, qrows, :] = l_b
            return carry

        lax.fori_loop(0, n_blocks // ATTN_UNROLL, branch_body, 0)

    def merge_body(n, carry):
        rows = pl.ds(pl.multiple_of(n * blk, blk), blk)
        ms = [m_sc[bi, rows, :] for bi in range(len(DILATIONS))]
        m_all = functools.reduce(jnp.maximum, ms)
        den = jnp.zeros((blk, LANES), F32)
        num = jnp.zeros((blk, LANES), F32)
        for bi in range(len(DILATIONS)):
            w = jnp.exp(ms[bi] - m_all)
            den = den + w * l_sc[bi, rows, :]
            num = num + w * acc_sc[bi, rows, :]
        o_ref[rows, :] = num / den
        return carry

    lax.fori_loop(0, n_blocks, merge_body, 0)


def _attention(qkv, bias):
    B, S, _ = qkv.shape
    n_pairs = ATTN_HEADS // 2
    blockspec = lambda off: pl.BlockSpec((None, S, LANES), lambda b, p: (b, 0, p + off))
    return pl.pallas_call(
        functools.partial(_attn_kernel, seq=S),
        grid=(B, n_pairs),
        in_specs=[
            blockspec(0), blockspec(n_pairs), blockspec(2 * n_pairs),
            pl.BlockSpec((2, 2 * ATTN_BLOCK, 2 * ATTN_BLOCK), lambda b, p: (0, 0, 0)),
        ],
        out_specs=pl.BlockSpec((None, S, LANES), lambda b, p: (b, 0, p)),
        out_shape=jax.ShapeDtypeStruct((B, S, ATTN_WIDTH), F32),
        scratch_shapes=[pltpu.VMEM((len(DILATIONS), S, LANES), F32)] * 3,
        compiler_params=pltpu.CompilerParams(
            dimension_semantics=("arbitrary", "arbitrary"),
            vmem_limit_bytes=VMEM_LIMIT),
        name="dilated_attn",
    )(qkv, qkv, qkv, bias)


def _chunk_cumsum(x, row_in_chunk):
    shift = 1
    while shift < HGRN_CHUNK:
        moved = pltpu.roll(x, shift, axis=0)
        x = x + jnp.where(row_in_chunk >= shift, moved, 0.0)
        shift *= 2
    return x


def _hgrn_kernel(q_ref, f_ref, i_ref, g_ref, lbl_ref, gn_ref, o_ref, st_sc, *, seq):
    R = HGRN_ROWS
    C = HGRN_CHUNK
    logits = lbl_ref[...]
    e = jnp.exp(logits - jnp.max(logits, axis=0, keepdims=True))
    lb = e[0:1] / jnp.sum(e, axis=0, keepdims=True)
    gn = gn_ref[...]

    st_sc[...] = jnp.zeros_like(st_sc)
    row = lax.broadcasted_iota(jnp.int32, (R, R), 0)
    col = lax.broadcasted_iota(jnp.int32, (R, R), 1)
    causal = (row >= col) & ((row // C) == (col // C))
    row_in_chunk = lax.broadcasted_iota(jnp.int32, (R, HGRN_DIM), 0) % C

    def body(c, carry):
        rows = pl.ds(pl.multiple_of(c * R, R), R)
        q = q_ref[rows, :].astype(F32)
        f = f_ref[rows, :].astype(F32)
        iv = i_ref[rows, :].astype(BF16)
        g = g_ref[rows, :].astype(F32)
        qf = q * jax.nn.sigmoid(q)
        forget = lb + (1.0 - lb) * jax.nn.sigmoid(f)
        key = 1.0 - forget
        b = _chunk_cumsum(jnp.log(forget), row_in_chunk)
        b_end = jnp.concatenate(
            [jnp.broadcast_to(b[j * C + C - 1:j * C + C], (C, HGRN_DIM))
             for j in range(R // C)], axis=0)
        q_dec = (qf * jnp.exp(b)).astype(BF16)
        k_inv = (key * jnp.exp(-b)).astype(BF16)
        k_end = (key * jnp.exp(b_end - b)).astype(BF16)
        a = lax.dot_general(q_dec, k_inv, (((1,), (1,)), ((), ())),
                            preferred_element_type=F32)
        a = jnp.where(causal, a, 0.0).astype(BF16)
        o_intra = jnp.dot(a, iv, preferred_element_type=F32)
        outs = []
        for j in range(R // C):
            sl = slice(j * C, (j + 1) * C)
            st = st_sc[...]
            o_inter = lax.dot_general(q_dec[sl], st.astype(BF16),
                                      (((1,), (1,)), ((), ())),
                                      preferred_element_type=F32)
            u_t = lax.dot_general(iv[sl], k_end[sl], (((0,), (0,)), ((), ())),
                                  preferred_element_type=F32)
            decay = jnp.exp(b_end[j * C:j * C + 1])
            st_sc[...] = st * decay + u_t
            outs.append(o_intra[sl] + o_inter)
        o = jnp.concatenate(outs, axis=0)
        o = _rms(o, gn)
        o_ref[rows, :] = (o * (g * jax.nn.sigmoid(g))).astype(o_ref.dtype)
        return carry

    lax.fori_loop(0, seq // R, body, 0)


def _hgrn(hg, lb_logits, gn):
    B, S, _ = hg.shape
    H = HGRN_HEADS
    n_rows = lb_logits.shape[1]
    blockspec = lambda off: pl.BlockSpec((None, S, HGRN_DIM), lambda b, h: (b, 0, h + off))
    return pl.pallas_call(
        functools.partial(_hgrn_kernel, seq=S),
        grid=(B, H),
        in_specs=[
            blockspec(0), blockspec(H), blockspec(2 * H), blockspec(3 * H),
            pl.BlockSpec((None, n_rows, HGRN_DIM), lambda b, h: (h, 0, 0)),
            pl.BlockSpec((None, 1, HGRN_DIM), lambda b, h: (h, 0, 0)),
        ],
        out_specs=pl.BlockSpec((None, S, HGRN_DIM), lambda b, h: (b, 0, h)),
        out_shape=jax.ShapeDtypeStruct((B, S, HGRN_WIDTH), BF16),
        scratch_shapes=[pltpu.VMEM((HGRN_DIM, HGRN_DIM), F32)],
        compiler_params=pltpu.CompilerParams(
            dimension_semantics=("arbitrary", "arbitrary"),
            vmem_limit_bytes=VMEM_LIMIT),
        name="hgrn2",
    )(hg, hg, hg, hg, lb_logits, gn)


def _ffn_kernel(x_ref, attn_ref, rec_ref, ga_ref, wout_ref, g2_ref, wup_ref,
                cw_ref, cb_ref, wdown_ref, gf_ref, o_ref,
                h_sc, u_sc, act_sc, carry_sc, *, tiles_per_seq):
    tm = x_ref.shape[0]
    i = pl.program_id(0)

    @pl.when(i % tiles_per_seq == 0)
    def _():
        carry_sc[...] = jnp.zeros_like(carry_sc)

    an = _rms(attn_ref[...], ga_ref[...]).astype(BF16)
    mix = jnp.dot(an, wout_ref[:ATTN_WIDTH, :], preferred_element_type=F32)
    mix = mix + jnp.dot(rec_ref[...], wout_ref[ATTN_WIDTH:, :],
                        preferred_element_type=F32)
    h = x_ref[...] + mix
    h_sc[...] = h
    u_sc[...] = _rms(h, g2_ref[...]).astype(BF16)

    row = lax.broadcasted_iota(jnp.int32, (tm, FF_CHUNK), 0)
    sqrt_half = math.sqrt(0.5)

    def chunk(c):
        u = u_sc[...]
        gate = jnp.dot(u, wup_ref[0, c], preferred_element_type=F32)
        val = jnp.dot(u, wup_ref[1, c], preferred_element_type=F32)
        prev = carry_sc[c]
        carry_sc[c] = gate[tm - SUBLANES:, :]
        p1 = jnp.broadcast_to(prev[SUBLANES - 1:SUBLANES], (tm, FF_CHUNK))
        p2 = jnp.broadcast_to(prev[SUBLANES - 2:SUBLANES - 1], (tm, FF_CHUNK))
        g1 = jnp.where(row == 0, p1, pltpu.roll(gate, 1, axis=0))
        g2 = jnp.where(row == 0, p2, jnp.where(row == 1, p1,
                                               pltpu.roll(gate, 2, axis=0)))
        cw = cw_ref[c]
        conv = cb_ref[c] + cw[0:1] * g2 + cw[1:2] * g1 + cw[2:3] * gate
        act = 0.5 * conv * (1.0 + lax.erf(conv * sqrt_half)) * val
        act_sc[:, c * FF_CHUNK:(c + 1) * FF_CHUNK] = act.astype(BF16)

    for c in range(N_FF_CHUNKS):
        chunk(c)
    down = jnp.dot(act_sc[...], wdown_ref[...], preferred_element_type=F32)
    o_ref[...] = _rms(h_sc[...] + down, gf_ref[...])


def _ffn(x2, attn2, rec2, ga, wout, g2, wup, cw, cb, wdown, gf, tm, seq):
    T = x2.shape[0]
    const = lambda shape: pl.BlockSpec(shape, lambda i: (0,) * len(shape),
                                       pipeline_mode=pl.Buffered(1))
    return pl.pallas_call(
        functools.partial(_ffn_kernel, tiles_per_seq=seq // tm),
        grid=(T // tm,),
        in_specs=[
            pl.BlockSpec((tm, D_MODEL), lambda i: (i, 0)),
            pl.BlockSpec((tm, ATTN_WIDTH), lambda i: (i, 0)),
            pl.BlockSpec((tm, HGRN_WIDTH), lambda i: (i, 0)),
            const((1, ATTN_WIDTH)),
            const((D_MODEL, D_MODEL)),
            const((1, D_MODEL)),
            const((2, N_FF_CHUNKS, D_MODEL, FF_CHUNK)),
            const((N_FF_CHUNKS, CONV_WIDTH, FF_CHUNK)),
            const((N_FF_CHUNKS, 1, FF_CHUNK)),
            const((D_FF, D_MODEL)),
            const((1, D_MODEL)),
        ],
        out_specs=pl.BlockSpec((tm, D_MODEL), lambda i: (i, 0)),
        out_shape=jax.ShapeDtypeStruct((T, D_MODEL), F32),
        scratch_shapes=[
            pltpu.VMEM((tm, D_MODEL), F32),
            pltpu.VMEM((tm, D_MODEL), BF16),
            pltpu.VMEM((tm, D_FF), BF16),
            pltpu.VMEM((N_FF_CHUNKS, SUBLANES, FF_CHUNK), F32),
        ],
        compiler_params=pltpu.CompilerParams(
            dimension_semantics=("arbitrary",), vmem_limit_bytes=VMEM_LIMIT),
        name="outproj_convglu",
    )(x2, attn2, rec2, ga, wout, g2, wup, cw, cb, wdown, gf)


def kernel(x, norm1_g, w_in, attn_norm_g, hgrn_norm_g, hgrn_lb_logits, w_out,
           norm2_g, w_up, conv_w, conv_b, w_down, final_norm_g):
    B, S, D = x.shape
    depth = w_in.shape[0]
    assert depth == 1 and D == D_MODEL
    assert S % (ATTN_BLOCK * DILATIONS[-1]) == 0 and S % HGRN_ROWS == 0
    tm = 512
    assert S % tm == 0
    T = B * S
    layer = 0
    x2 = x.reshape(T, D)

    qkv, hg = _inproj(x2, norm1_g[layer].reshape(1, D), w_in[layer].astype(BF16), tm)

    attn = _attention(qkv.reshape(B, S, QKV_WIDTH), _attn_bias())

    lbl = hgrn_lb_logits.astype(F32).reshape(depth + 1, HGRN_HEADS, HGRN_DIM)
    lbl = lbl.transpose(1, 0, 2)
    gn = hgrn_norm_g[layer].reshape(HGRN_HEADS, 1, HGRN_DIM)
    rec = _hgrn(hg.reshape(B, S, HG_WIDTH), lbl, gn)

    wup = w_up[layer].astype(BF16).reshape(D, 2, N_FF_CHUNKS, FF_CHUNK).transpose(1, 2, 0, 3)
    cw = conv_w[layer].reshape(CONV_WIDTH, N_FF_CHUNKS, FF_CHUNK).transpose(1, 0, 2)
    cb = conv_b[layer].reshape(N_FF_CHUNKS, 1, FF_CHUNK)
    wdown = w_down[layer].astype(BF16)
    out = _ffn(x2, attn.reshape(T, ATTN_WIDTH), rec.reshape(T, HGRN_WIDTH),
               attn_norm_g[layer].reshape(1, ATTN_WIDTH), w_out[layer].astype(BF16),
               norm2_g[layer].reshape(1, D), wup, cw, cb, wdown,
               final_norm_g.reshape(1, D), tm, S)
    return out.reshape(B, S, D)
```

```python
import functools
import math

import jax
import jax.numpy as jnp
from jax import lax
from jax.experimental import pallas as pl
from jax.experimental.pallas import tpu as pltpu

F32 = jnp.float32
BF16 = jnp.bfloat16

D_MODEL = 1024
ATTN_WIDTH = 512
ATTN_HEADS = 8
ATTN_HEAD_DIM = 64
ATTN_BLOCK = 128
DILATIONS = (1, 4, 16)
HGRN_WIDTH = 512
HGRN_HEADS = 4
HGRN_DIM = 128
HGRN_CHUNK = 64
D_FF = 2816
CONV_WIDTH = 3
NORM_EPS = 1e-6
QKV_WIDTH = 3 * ATTN_WIDTH
HG_WIDTH = 4 * HGRN_WIDTH

LANES = 128
SUBLANES = 8
FF_CHUNK = 256
N_FF_CHUNKS = D_FF // FF_CHUNK
HGRN_ROWS = 256
HGRN_GROUP = 8
ATTN_UNROLL = 16
VMEM_LIMIT = 56 * 1024 * 1024


def _rms(x, g):
    return x * lax.rsqrt(jnp.mean(x * x, axis=-1, keepdims=True) + NORM_EPS) * g


def _inproj_kernel(x_ref, g_ref, w_ref, qkv_ref, hg_ref):
    u = _rms(x_ref[...], g_ref[...]).astype(BF16)
    qkv_ref[...] = jnp.dot(u, w_ref[:, :QKV_WIDTH], preferred_element_type=F32)
    hg_ref[...] = jnp.dot(u, w_ref[:, QKV_WIDTH:],
                          preferred_element_type=F32).astype(hg_ref.dtype)


def _inproj(x2, g, w, tm):
    T = x2.shape[0]
    return pl.pallas_call(
        _inproj_kernel,
        grid=(T // tm,),
        in_specs=[
            pl.BlockSpec((tm, D_MODEL), lambda i: (i, 0)),
            pl.BlockSpec((1, D_MODEL), lambda i: (0, 0)),
            pl.BlockSpec((D_MODEL, QKV_WIDTH + HG_WIDTH), lambda i: (0, 0),
                         pipeline_mode=pl.Buffered(1)),
        ],
        out_specs=[
            pl.BlockSpec((tm, QKV_WIDTH), lambda i: (i, 0)),
            pl.BlockSpec((tm, HG_WIDTH), lambda i: (i, 0)),
        ],
        out_shape=[
            jax.ShapeDtypeStruct((T, QKV_WIDTH), F32),
            jax.ShapeDtypeStruct((T, HG_WIDTH), BF16),
        ],
        compiler_params=pltpu.CompilerParams(
            dimension_semantics=("arbitrary",), vmem_limit_bytes=VMEM_LIMIT),
        name="inproj",
    )(x2, g, w)


def _attn_bias():
    blk = ATTN_BLOCK
    qi = jnp.arange(2 * blk)[:, None] % blk
    kj = jnp.arange(2 * blk)[None, :]
    generic = jnp.where(kj < blk, kj >= qi, (kj - blk) <= qi)
    first = (kj < blk) & (kj <= qi)
    valid = jnp.stack([generic, first])
    return jnp.where(valid, 0.0, -jnp.inf).astype(F32)


def _attn_kernel(q_ref, k_ref, v_ref, bias_ref, o_ref, m_sc, l_sc, acc_sc, *, seq):
    blk = ATTN_BLOCK
    n_blocks = seq // blk
    head0 = lax.broadcasted_iota(jnp.int32, (blk, LANES), 1) < ATTN_HEAD_DIM
    scale = ATTN_HEAD_DIM ** -0.5

    def block_softmax(qrows, krows, first):
        q2 = q_ref[qrows, :] * scale
        zero = jnp.zeros_like(q2)
        qs = jnp.concatenate([jnp.where(head0, q2, zero),
                              jnp.where(head0, zero, q2)], axis=0).astype(BF16)
        kw = k_ref[krows, :].astype(BF16)
        vw = v_ref[krows, :].astype(BF16)
        s = lax.dot_general(qs, kw, (((1,), (1,)), ((), ())),
                            preferred_element_type=F32)
        s = s + bias_ref[first]
        m = jnp.max(s, axis=-1, keepdims=True)
        p = jnp.exp(s - m)
        l = jnp.sum(p, axis=-1, keepdims=True)
        pv = jnp.dot(p.astype(BF16), vw, preferred_element_type=F32)
        o_b = jnp.where(head0, pv[:blk], pv[blk:])
        m_b = jnp.where(head0, jnp.broadcast_to(m[:blk], (blk, LANES)),
                        jnp.broadcast_to(m[blk:], (blk, LANES)))
        l_b = jnp.where(head0, jnp.broadcast_to(l[:blk], (blk, LANES)),
                        jnp.broadcast_to(l[blk:], (blk, LANES)))
        return o_b, m_b, l_b

    for bi, d in enumerate(DILATIONS):
        nb = n_blocks // d

        def branch_body(it, carry, bi=bi, d=d, nb=nb):
            for g in range(ATTN_UNROLL):
                j = it * ATTN_UNROLL + g
                r = j // nb
                n = j % nb
                first = jnp.where(n == 0, 1, 0)
                if d == 1:
                    qrows = pl.ds(pl.multiple_of(n * blk, blk), blk)
                    krows = pl.ds(pl.multiple_of(jnp.maximum(n - 1, 0) * blk, blk), 2 * blk)
                else:
                    qrows = pl.ds(n * (blk * d) + r, blk, stride=d)
                    krows = pl.ds(jnp.maximum(n - 1, 0) * (blk * d) + r, 2 * blk, stride=d)
                o_b, m_b, l_b = block_softmax(qrows, krows, first)
                acc_sc[bi, qrows, :] = o_b
                m_sc[bi, qrows, :] = m_b
                l_sc[bi, qrows, :] = l_b
            return carry

        lax.fori_loop(0, n_blocks // ATTN_UNROLL, branch_body, 0)

    def merge_body(n, carry):
        rows = pl.ds(pl.multiple_of(n * blk, blk), blk)
        ms = [m_sc[bi, rows, :] for bi in range(len(DILATIONS))]
        m_all = functools.reduce(jnp.maximum, ms)
        den = jnp.zeros((blk, LANES), F32)
        num = jnp.zeros((blk, LANES), F32)
        for bi in range(len(DILATIONS)):
            w = jnp.exp(ms[bi] - m_all)
            den = den + w * l_sc[bi, rows, :]
            num = num + w * acc_sc[bi, rows, :]
        o_ref[rows, :] = num / den
        return carry

    lax.fori_loop(0, n_blocks, merge_body, 0)


def _attention(qkv, bias):
    B, S, _ = qkv.shape
    n_pairs = ATTN_HEADS // 2
    blockspec = lambda off: pl.BlockSpec((None, S, LANES), lambda b, p: (b, 0, p + off))
    return pl.pallas_call(
        functools.partial(_attn_kernel, seq=S),
        grid=(B, n_pairs),
        in_specs=[
            blockspec(0), blockspec(n_pairs), blockspec(2 * n_pairs),
            pl.BlockSpec((2, 2 * ATTN_BLOCK, 2 * ATTN_BLOCK), lambda b, p: (0, 0, 0)),
        ],
        out_specs=pl.BlockSpec((None, S, LANES), lambda b, p: (b, 0, p)),
        out_shape=jax.ShapeDtypeStruct((B, S, ATTN_WIDTH), F32),
        scratch_shapes=[pltpu.VMEM((len(DILATIONS), S, LANES), F32)] * 3,
        compiler_params=pltpu.CompilerParams(
            dimension_semantics=("arbitrary", "arbitrary"),
            vmem_limit_bytes=VMEM_LIMIT),
        name="dilated_attn",
    )(qkv, qkv, qkv, bias)


def _hgrn_kernel(q_ref, f_ref, i_ref, g_ref, lbl_ref, gn_ref, o_ref, st_sc, tri_sc, *, seq):
    R = HGRN_ROWS
    C = HGRN_CHUNK
    logits = lbl_ref[...]
    e = jnp.exp(logits - jnp.max(logits, axis=0, keepdims=True))
    lb = e[0:1] / jnp.sum(e, axis=0, keepdims=True)
    f_mid = 0.5 * (1.0 + lb)
    f_half = 0.5 * (1.0 - lb)
    gn = gn_ref[...]

    st_sc[...] = jnp.zeros_like(st_sc)
    row = lax.broadcasted_iota(jnp.int32, (R, R), 0)
    col = lax.broadcasted_iota(jnp.int32, (R, R), 1)
    causal = (row >= col) & ((row // C) == (col // C))
    tri_sc[...] = jnp.where(causal, 1.0, 0.0).astype(BF16)

    G = HGRN_GROUP
    NC = R // C

    def body(it, carry):
        rows = [pl.ds(pl.multiple_of((it * G + g) * R, R), R) for g in range(G)]
        ivs, keys, qfs, b2s = [], [], [], []
        for g in range(G):
            q = q_ref[rows[g], :].astype(F32)
            f = f_ref[rows[g], :].astype(F32)
            ivs.append(i_ref[rows[g], :].astype(BF16))
            hq = 0.5 * q
            qfs.append(hq + hq * jnp.tanh(hq))
            forget = f_mid + f_half * jnp.tanh(0.5 * f)
            keys.append(1.0 - forget)
            lf = jnp.log2(forget)
            lf_hi = lf.astype(BF16)
            lf_lo = (lf - lf_hi.astype(F32)).astype(BF16)
            b2s.append(jnp.dot(tri_sc[...], jnp.concatenate([lf_hi, lf_lo], axis=1),
                               preferred_element_type=F32))
        q_decs, k_ends, decays, a_s = [], [], [], []
        for g in range(G):
            b = b2s[g][:, :HGRN_DIM] + b2s[g][:, HGRN_DIM:]
            decay_rows = [jnp.exp2(b[j * C + C - 1:j * C + C]) for j in range(NC)]
            decay_end = jnp.concatenate(
                [jnp.broadcast_to(d, (C, HGRN_DIM)) for d in decay_rows], axis=0)
            k_inv_f = keys[g] * jnp.exp2(-b)
            q_dec = (qfs[g] * jnp.exp2(b)).astype(BF16)
            k_ends.append((k_inv_f * decay_end).astype(BF16))
            q_decs.append(q_dec)
            decays.append(decay_rows)
            a_s.append(lax.dot_general(q_dec, k_inv_f.astype(BF16),
                                       (((1,), (1,)), ((), ())),
                                       preferred_element_type=F32))
        o_intras, u_ts = [], []
        for g in range(G):
            a = jnp.where(causal, a_s[g], 0.0).astype(BF16)
            o_intras.append(jnp.dot(a, ivs[g], preferred_element_type=F32))
            u_ts.append([lax.dot_general(ivs[g][j * C:(j + 1) * C],
                                         k_ends[g][j * C:(j + 1) * C],
                                         (((0,), (0,)), ((), ())),
                                         preferred_element_type=F32)
                         for j in range(NC)])
        st = st_sc[...]
        o_inters = []
        for g in range(G):
            parts = []
            for j in range(NC):
                parts.append(lax.dot_general(q_decs[g][j * C:(j + 1) * C], st.astype(BF16),
                                             (((1,), (1,)), ((), ())),
                                             preferred_element_type=F32))
                st = st * decays[g][j] + u_ts[g][j]
            o_inters.append(jnp.concatenate(parts, axis=0))
        st_sc[...] = st
        for g in range(G):
            o = _rms(o_intras[g] + o_inters[g], gn)
            hg = 0.5 * g_ref[rows[g], :].astype(F32)
            o_ref[rows[g], :] = (o * (hg + hg * jnp.tanh(hg))).astype(o_ref.dtype)
        return carry

    lax.fori_loop(0, seq // (R * G), body, 0)


def _hgrn(hg, lb_logits, gn):
    B, S, _ = hg.shape
    H = HGRN_HEADS
    n_rows = lb_logits.shape[1]
    blockspec = lambda off: pl.BlockSpec((None, S, HGRN_DIM), lambda b, h: (b, 0, h + off))
    return pl.pallas_call(
        functools.partial(_hgrn_kernel, seq=S),
        grid=(B, H),
        in_specs=[
            blockspec(0), blockspec(H), blockspec(2 * H), blockspec(3 * H),
            pl.BlockSpec((None, n_rows, HGRN_DIM), lambda b, h: (h, 0, 0)),
            pl.BlockSpec((None, 1, HGRN_DIM), lambda b, h: (h, 0, 0)),
        ],
        out_specs=pl.BlockSpec((None, S, HGRN_DIM), lambda b, h: (b, 0, h)),
        out_shape=jax.ShapeDtypeStruct((B, S, HGRN_WIDTH), BF16),
        scratch_shapes=[pltpu.VMEM((HGRN_DIM, HGRN_DIM), F32),
                        pltpu.VMEM((HGRN_ROWS, HGRN_ROWS), BF16)],
        compiler_params=pltpu.CompilerParams(
            dimension_semantics=("arbitrary", "arbitrary"),
            vmem_limit_bytes=VMEM_LIMIT),
        name="hgrn2",
    )(hg, hg, hg, hg, lb_logits, gn)


def _ffn_kernel(x_ref, attn_ref, rec_ref, ga_ref, wout_ref, g2_ref, wup_ref,
                cw_ref, cb_ref, wdown_ref, gf_ref, o_ref,
                h_sc, u_sc, act_sc, carry_sc, *, tiles_per_seq):
    tm = x_ref.shape[0]
    i = pl.program_id(0)

    @pl.when(i % tiles_per_seq == 0)
    def _():
        carry_sc[...] = jnp.zeros_like(carry_sc)

    an = _rms(attn_ref[...], ga_ref[...]).astype(BF16)
    mix = jnp.dot(an, wout_ref[:ATTN_WIDTH, :], preferred_element_type=F32)
    mix = mix + jnp.dot(rec_ref[...], wout_ref[ATTN_WIDTH:, :],
                        preferred_element_type=F32)
    h = x_ref[...] + mix
    h_sc[...] = h
    u_sc[...] = _rms(h, g2_ref[...]).astype(BF16)

    row = lax.broadcasted_iota(jnp.int32, (tm, FF_CHUNK), 0)
    sqrt_half = math.sqrt(0.5)

    def chunk(c):
        u = u_sc[...]
        gate = jnp.dot(u, wup_ref[0, c], preferred_element_type=F32)
        val = jnp.dot(u, wup_ref[1, c], preferred_element_type=F32)
        prev = carry_sc[c]
        carry_sc[c] = gate[tm - SUBLANES:, :]
        p1 = jnp.broadcast_to(prev[SUBLANES - 1:SUBLANES], (tm, FF_CHUNK))
        p2 = jnp.broadcast_to(prev[SUBLANES - 2:SUBLANES - 1], (tm, FF_CHUNK))
        g1 = jnp.where(row == 0, p1, pltpu.roll(gate, 1, axis=0))
        g2 = jnp.where(row == 0, p2, jnp.where(row == 1, p1,
                                               pltpu.roll(gate, 2, axis=0)))
        cw = cw_ref[c]
        conv = cb_ref[c] + cw[0:1] * g2 + cw[1:2] * g1 + cw[2:3] * gate
        act = 0.5 * conv * (1.0 + lax.erf(conv * sqrt_half)) * val
        act_sc[:, c * FF_CHUNK:(c + 1) * FF_CHUNK] = act.astype(BF16)

    for c in range(N_FF_CHUNKS):
        chunk(c)
    down = jnp.dot(act_sc[...], wdown_ref[...], preferred_element_type=F32)
    o_ref[...] = _rms(h_sc[...] + down, gf_ref[...])


def _ffn(x2, attn2, rec2, ga, wout, g2, wup, cw, cb, wdown, gf, tm, seq):
    T = x2.shape[0]
    const = lambda shape: pl.BlockSpec(shape, lambda i: (0,) * len(shape),
                                       pipeline_mode=pl.Buffered(1))
    return pl.pallas_call(
        functools.partial(_ffn_kernel, tiles_per_seq=seq // tm),
        grid=(T // tm,),
        in_specs=[
            pl.BlockSpec((tm, D_MODEL), lambda i: (i, 0)),
            pl.BlockSpec((tm, ATTN_WIDTH), lambda i: (i, 0)),
            pl.BlockSpec((tm, HGRN_WIDTH), lambda i: (i, 0)),
            const((1, ATTN_WIDTH)),
            const((D_MODEL, D_MODEL)),
            const((1, D_MODEL)),
            const((2, N_FF_CHUNKS, D_MODEL, FF_CHUNK)),
            const((N_FF_CHUNKS, CONV_WIDTH, FF_CHUNK)),
            const((N_FF_CHUNKS, 1, FF_CHUNK)),
            const((D_FF, D_MODEL)),
            const((1, D_MODEL)),
        ],
        out_specs=pl.BlockSpec((tm, D_MODEL), lambda i: (i, 0)),
        out_shape=jax.ShapeDtypeStruct((T, D_MODEL), F32),
        scratch_shapes=[
            pltpu.VMEM((tm, D_MODEL), F32),
            pltpu.VMEM((tm, D_MODEL), BF16),
            pltpu.VMEM((tm, D_FF), BF16),
            pltpu.VMEM((N_FF_CHUNKS, SUBLANES, FF_CHUNK), F32),
        ],
        compiler_params=pltpu.CompilerParams(
            dimension_semantics=("arbitrary",), vmem_limit_bytes=VMEM_LIMIT),
        name="outproj_convglu",
    )(x2, attn2, rec2, ga, wout, g2, wup, cw, cb, wdown, gf)


def kernel(x, norm1_g, w_in, attn_norm_g, hgrn_norm_g, hgrn_lb_logits, w_out,
           norm2_g, w_up, conv_w, conv_b, w_down, final_norm_g):
    B, S, D = x.shape
    depth = w_in.shape[0]
    assert depth == 1 and D == D_MODEL
    assert S % (ATTN_BLOCK * DILATIONS[-1]) == 0 and S % HGRN_ROWS == 0
    tm = 512
    assert S % tm == 0
    T = B * S
    layer = 0
    x2 = x.reshape(T, D)

    qkv, hg = _inproj(x2, norm1_g[layer].reshape(1, D), w_in[layer].astype(BF16), tm)

    attn = _attention(qkv.reshape(B, S, QKV_WIDTH), _attn_bias())

    lbl = hgrn_lb_logits.astype(F32).reshape(depth + 1, HGRN_HEADS, HGRN_DIM)
    lbl = lbl.transpose(1, 0, 2)
    gn = hgrn_norm_g[layer].reshape(HGRN_HEADS, 1, HGRN_DIM)
    rec = _hgrn(hg.reshape(B, S, HG_WIDTH), lbl, gn)

    wup = w_up[layer].astype(BF16).reshape(D, 2, N_FF_CHUNKS, FF_CHUNK).transpose(1, 2, 0, 3)
    cw = conv_w[layer].reshape(CONV_WIDTH, N_FF_CHUNKS, FF_CHUNK).transpose(1, 0, 2)
    cb = conv_b[layer].reshape(N_FF_CHUNKS, 1, FF_CHUNK)
    wdown = w_down[layer].astype(BF16)
    out = _ffn(x2, attn.reshape(T, ATTN_WIDTH), rec.reshape(T, HGRN_WIDTH),
               attn_norm_g[layer].reshape(1, ATTN_WIDTH), w_out[layer].astype(BF16),
               norm2_g[layer].reshape(1, D), wup, cw, cb, wdown,
               final_norm_g.reshape(1, D), tm, S)
    return out.reshape(B, S, D)
```

```python
import functools
import math

import jax
import jax.numpy as jnp
from jax import lax
from jax.experimental import pallas as pl
from jax.experimental.pallas import tpu as pltpu

F32 = jnp.float32
BF16 = jnp.bfloat16

D_MODEL = 1024
ATTN_WIDTH = 512
ATTN_HEADS = 8
ATTN_HEAD_DIM = 64
ATTN_BLOCK = 128
DILATIONS = (1, 4, 16)
ATTN_RESIDUES = 16
HGRN_WIDTH = 512
HGRN_HEADS = 4
HGRN_DIM = 128
HGRN_CHUNK = 64
D_FF = 2816
CONV_WIDTH = 3
NORM_EPS = 1e-6
QKV_WIDTH = 3 * ATTN_WIDTH
ATT_OPS_WIDTH = 5 * ATTN_WIDTH
HG_WIDTH = 4 * HGRN_WIDTH

LANES = 128
SUBLANES = 8
FF_CHUNK = 256
N_FF_CHUNKS = D_FF // FF_CHUNK
HGRN_ROWS = 256
HGRN_GROUP = 8
ATTN_GROUP = 16
VMEM_LIMIT = 56 * 1024 * 1024


def _rms(x, g):
    return x * lax.rsqrt(jnp.mean(x * x, axis=-1, keepdims=True) + NORM_EPS) * g


def _inproj_kernel(x_ref, g_ref, w_ref, att_ref, hg_ref, res_sc, tmp_sc):
    tm = x_ref.shape[0]
    run = tm // ATTN_RESIDUES
    quarter = tm // 4
    u = _rms(x_ref[...], g_ref[...]).astype(BF16)
    for c in range(QKV_WIDTH // FF_CHUNK):
        r = jnp.dot(u, w_ref[:, c * FF_CHUNK:(c + 1) * FF_CHUNK],
                    preferred_element_type=F32)
        res_sc[2 * c] = r[:, :LANES]
        res_sc[2 * c + 1] = r[:, LANES:]
    hg_ref[...] = jnp.dot(u, w_ref[:, QKV_WIDTH:],
                          preferred_element_type=F32).astype(hg_ref.dtype)
    head0 = lax.broadcasted_iota(jnp.int32, (run, LANES), 1) < ATTN_HEAD_DIM
    scale = ATTN_HEAD_DIM ** -0.5
    zero = jnp.zeros((run, LANES), F32)
    one = jnp.ones((run, LANES), F32)
    pairs = ATTN_WIDTH // LANES
    for s in range(QKV_WIDTH // LANES):
        kind, p = divmod(s, pairs)
        cols = lambda group: slice((group * pairs + p) * LANES, (group * pairs + p + 1) * LANES)
        for r4 in range(4):
            tmp_sc[s, r4 * quarter:(r4 + 1) * quarter, :] = res_sc[s, pl.ds(r4, quarter, stride=4), :]
        for r16 in range(ATTN_RESIDUES):
            rows = tmp_sc[s, pl.ds((r16 % 4) * quarter + r16 // 4, run, stride=4), :]
            if kind == 0:
                q = rows * scale
                att_ref[r16, :, cols(0)] = jnp.where(head0, q, zero)
                att_ref[r16, :, cols(1)] = jnp.where(head0, zero, q)
            elif kind == 1:
                att_ref[r16, :, cols(2)] = rows
            else:
                att_ref[r16, :, cols(3)] = jnp.where(head0, rows, one)
                att_ref[r16, :, cols(4)] = jnp.where(head0, one, rows)


def _inproj(x2, g, w, tm, seq):
    T = x2.shape[0]
    tps = seq // tm
    run = tm // ATTN_RESIDUES
    return pl.pallas_call(
        _inproj_kernel,
        grid=(T // tm,),
        in_specs=[
            pl.BlockSpec((tm, D_MODEL), lambda i: (i, 0)),
            pl.BlockSpec((1, D_MODEL), lambda i: (0, 0)),
            pl.BlockSpec((D_MODEL, QKV_WIDTH + HG_WIDTH), lambda i: (0, 0),
                         pipeline_mode=pl.Buffered(1)),
        ],
        out_specs=[
            pl.BlockSpec((None, ATTN_RESIDUES, run, ATT_OPS_WIDTH),
                         lambda i: (i // tps, 0, i % tps, 0)),
            pl.BlockSpec((tm, HG_WIDTH), lambda i: (i, 0)),
        ],
        out_shape=[
            jax.ShapeDtypeStruct((T // seq, ATTN_RESIDUES, seq // ATTN_RESIDUES,
                                  ATT_OPS_WIDTH), F32),
            jax.ShapeDtypeStruct((T, HG_WIDTH), BF16),
        ],
        scratch_shapes=[pltpu.VMEM((QKV_WIDTH // LANES, tm, LANES), F32)] * 2,
        compiler_params=pltpu.CompilerParams(
            dimension_semantics=("arbitrary",), vmem_limit_bytes=VMEM_LIMIT),
        name="inproj",
    )(x2, g, w)


def _attn_runs(d):
    g = ATTN_RESIDUES // d
    return g, ATTN_BLOCK // g


def _attn_bias():
    blk = ATTN_BLOCK
    e = jnp.arange(2 * blk) % blk
    half = jnp.arange(2 * blk) // blk
    out = []
    for d in DILATIONS:
        g, run = _attn_runs(d)
        pos = g * (e % run) + e // run
        pos_q = pos[:, None]
        pos_k = (blk * half + pos)[None, :]
        dist = blk + pos_q - pos_k
        out.append((dist >= 0) & (dist <= blk))
        out.append(pos_k <= pos_q)
    return jnp.where(jnp.stack(out), 0.0, -jnp.inf).astype(F32)


def _attn_kernel(q0_ref, q1_ref, k_ref, v0_ref, v1_ref, bias_ref, o_ref,
                 m_sc, l_sc, acc_sc, *, seq):
    blk = ATTN_BLOCK
    n_blocks = seq // blk
    ns = seq // ATTN_RESIDUES
    head0 = lax.broadcasted_iota(jnp.int32, (blk, LANES), 1) < ATTN_HEAD_DIM

    def gather(ref, runs):
        return jnp.concatenate([ref[rs, :] for rs in runs], axis=0)

    def block_runs(d, j, which):
        nb = n_blocks // d
        g, run = _attn_runs(d)
        r, n = divmod(j, nb)
        n0 = max(n - 1, 0)
        nsel = {"q": [n], "kv": [n0, n0 + 1]}[which]
        return [pl.ds((d * jj + r) * ns + run * nn, run) for nn in nsel for jj in range(g)]

    def scores_of(group):
        out = []
        for d, j in group:
            qruns = block_runs(d, j, "q")
            qs = jnp.concatenate([gather(q0_ref, qruns),
                                  gather(q1_ref, qruns)], axis=0).astype(BF16)
            kw = gather(k_ref, block_runs(d, j, "kv")).astype(BF16)
            out.append(lax.dot_general(qs, kw, (((1,), (1,)), ((), ())),
                                       preferred_element_type=F32))
        return out

    def finish(group, scores):
        parts = []
        for (d, j), s in zip(group, scores):
            kruns = block_runs(d, j, "kv")
            first = 1 if j % (n_blocks // d) == 0 else 0
            s = s + bias_ref[2 * DILATIONS.index(d) + first]
            m = jnp.max(s, axis=-1, keepdims=True)
            p = jnp.exp(s - m).astype(BF16)
            pv0 = jnp.dot(p[:blk], gather(v0_ref, kruns).astype(BF16),
                          preferred_element_type=F32)
            pv1 = jnp.dot(p[blk:], gather(v1_ref, kruns).astype(BF16),
                          preferred_element_type=F32)
            parts.append((pv0, pv1, m))
        for (d, j), (pv0, pv1, m) in zip(group, parts):
            bi = DILATIONS.index(d)
            qruns = block_runs(d, j, "q")
            o_b = jnp.where(head0, pv0, pv1)
            l_b = jnp.where(head0, pv1, pv0)
            m_b = jnp.where(head0, jnp.broadcast_to(m[:blk], (blk, LANES)),
                            jnp.broadcast_to(m[blk:], (blk, LANES)))
            run_len = blk // len(qruns)
            for i, rs in enumerate(qruns):
                piece = slice(i * run_len, (i + 1) * run_len)
                acc_sc[bi, rs, :] = o_b[piece]
                m_sc[bi, rs, :] = m_b[piece]
                l_sc[bi, rs, :] = l_b[piece]

    blocks = [(d, j) for d in DILATIONS for j in range(n_blocks)]
    groups = [blocks[i:i + ATTN_GROUP] for i in range(0, len(blocks), ATTN_GROUP)]
    scores = scores_of(groups[0])
    for i, group in enumerate(groups):
        nxt = scores_of(groups[i + 1]) if i + 1 < len(groups) else None
        finish(group, scores)
        scores = nxt

    def merge_body(t, carry):
        rows = pl.ds(pl.multiple_of(t * blk, blk), blk)
        ms = [m_sc[bi, rows, :] for bi in range(len(DILATIONS))]
        m_all = functools.reduce(jnp.maximum, ms)
        den = jnp.zeros((blk, LANES), F32)
        num = jnp.zeros((blk, LANES), F32)
        for bi in range(len(DILATIONS)):
            w = jnp.exp(ms[bi] - m_all)
            den = den + w * pltpu.roll(l_sc[bi, rows, :], ATTN_HEAD_DIM, axis=1)
            num = num + w * acc_sc[bi, rows, :]
        o_ref[rows, :] = num / den
        return carry

    lax.fori_loop(0, n_blocks, merge_body, 0, unroll=4)


def _attention(att_ops, bias):
    B, S, _ = att_ops.shape
    n_pairs = ATTN_WIDTH // LANES
    blockspec = lambda group: pl.BlockSpec((None, S, LANES),
                                           lambda b, p: (b, 0, group * n_pairs + p))
    return pl.pallas_call(
        functools.partial(_attn_kernel, seq=S),
        grid=(B, n_pairs),
        in_specs=[blockspec(group) for group in range(ATT_OPS_WIDTH // ATTN_WIDTH)] + [
            pl.BlockSpec(bias.shape, lambda b, p: (0, 0, 0), pipeline_mode=pl.Buffered(1)),
        ],
        out_specs=pl.BlockSpec((None, S, LANES), lambda b, p: (b, 0, p)),
        out_shape=jax.ShapeDtypeStruct((B, S, ATTN_WIDTH), F32),
        scratch_shapes=[pltpu.VMEM((len(DILATIONS), S, LANES), F32)] * 3,
        compiler_params=pltpu.CompilerParams(
            dimension_semantics=("arbitrary", "arbitrary"),
            vmem_limit_bytes=VMEM_LIMIT),
        name="dilated_attn",
    )(*([att_ops] * (ATT_OPS_WIDTH // ATTN_WIDTH)), bias)


def _hgrn_kernel(q_ref, f_ref, i_ref, g_ref, lbl_ref, gn_ref, o_ref, st_sc, tri_sc, *, seq):
    R = HGRN_ROWS
    C = HGRN_CHUNK
    logits = lbl_ref[...]
    e = jnp.exp(logits - jnp.max(logits, axis=0, keepdims=True))
    lb = e[0:1] / jnp.sum(e, axis=0, keepdims=True)
    f_mid = 0.5 * (1.0 + lb)
    f_half = 0.5 * (1.0 - lb)
    gn = gn_ref[...]

    st_sc[...] = jnp.zeros_like(st_sc)
    row = lax.broadcasted_iota(jnp.int32, (R, R), 0)
    col = lax.broadcasted_iota(jnp.int32, (R, R), 1)
    causal = (row >= col) & ((row // C) == (col // C))
    tri_sc[...] = jnp.where(causal, 1.0, 0.0).astype(BF16)

    G = HGRN_GROUP
    NC = R // C

    def body(it, carry):
        rows = [pl.ds(pl.multiple_of((it * G + g) * R, R), R) for g in range(G)]
        ivs, keys, qfs, b2s = [], [], [], []
        for g in range(G):
            q = q_ref[rows[g], :].astype(F32)
            f = f_ref[rows[g], :].astype(F32)
            ivs.append(i_ref[rows[g], :].astype(BF16))
            hq = 0.5 * q
            qfs.append(hq + hq * jnp.tanh(hq))
            forget = f_mid + f_half * jnp.tanh(0.5 * f)
            keys.append(1.0 - forget)
            lf = jnp.log2(forget)
            lf_hi = lf.astype(BF16)
            lf_lo = (lf - lf_hi.astype(F32)).astype(BF16)
            b2s.append(jnp.dot(tri_sc[...], jnp.concatenate([lf_hi, lf_lo], axis=1),
                               preferred_element_type=F32))
        q_decs, k_ends, decays, a_s = [], [], [], []
        for g in range(G):
            b = b2s[g][:, :HGRN_DIM] + b2s[g][:, HGRN_DIM:]
            decay_rows = [jnp.exp2(b[j * C + C - 1:j * C + C]) for j in range(NC)]
            decay_end = jnp.concatenate(
                [jnp.broadcast_to(d, (C, HGRN_DIM)) for d in decay_rows], axis=0)
            k_inv_f = keys[g] * jnp.exp2(-b)
            q_dec = (qfs[g] * jnp.exp2(b)).astype(BF16)
            k_ends.append((k_inv_f * decay_end).astype(BF16))
            q_decs.append(q_dec)
            decays.append(decay_rows)
            a_s.append(lax.dot_general(q_dec, k_inv_f.astype(BF16),
                                       (((1,), (1,)), ((), ())),
                                       preferred_element_type=F32))
        o_intras, u_ts = [], []
        for g in range(G):
            a = jnp.where(causal, a_s[g], 0.0).astype(BF16)
            o_intras.append(jnp.dot(a, ivs[g], preferred_element_type=F32))
            u_ts.append([lax.dot_general(ivs[g][j * C:(j + 1) * C],
                                         k_ends[g][j * C:(j + 1) * C],
                                         (((0,), (0,)), ((), ())),
                                         preferred_element_type=F32)
                         for j in range(NC)])
        st = st_sc[...]
        o_inters = []
        for g in range(G):
            parts = []
            for j in range(NC):
                parts.append(lax.dot_general(q_decs[g][j * C:(j + 1) * C], st.astype(BF16),
                                             (((1,), (1,)), ((), ())),
                                             preferred_element_type=F32))
                st = st * decays[g][j] + u_ts[g][j]
            o_inters.append(jnp.concatenate(parts, axis=0))
        st_sc[...] = st
        for g in range(G):
            o = _rms(o_intras[g] + o_inters[g], gn)
            hg = 0.5 * g_ref[rows[g], :].astype(F32)
            o_ref[rows[g], :] = (o * (hg + hg * jnp.tanh(hg))).astype(o_ref.dtype)
        return carry

    lax.fori_loop(0, seq // (R * G), body, 0)


def _hgrn(hg, lb_logits, gn):
    B, S, _ = hg.shape
    H = HGRN_HEADS
    n_rows = lb_logits.shape[1]
    blockspec = lambda off: pl.BlockSpec((None, S, HGRN_DIM), lambda b, h: (b, 0, h + off))
    return pl.pallas_call(
        functools.partial(_hgrn_kernel, seq=S),
        grid=(B, H),
        in_specs=[
            blockspec(0), blockspec(H), blockspec(2 * H), blockspec(3 * H),
            pl.BlockSpec((None, n_rows, HGRN_DIM), lambda b, h: (h, 0, 0)),
            pl.BlockSpec((None, 1, HGRN_DIM), lambda b, h: (h, 0, 0)),
        ],
        out_specs=pl.BlockSpec((None, S, HGRN_DIM), lambda b, h: (b, 0, h)),
        out_shape=jax.ShapeDtypeStruct((B, S, HGRN_WIDTH), BF16),
        scratch_shapes=[pltpu.VMEM((HGRN_DIM, HGRN_DIM), F32),
                        pltpu.VMEM((HGRN_ROWS, HGRN_ROWS), BF16)],
        compiler_params=pltpu.CompilerParams(
            dimension_semantics=("arbitrary", "arbitrary"),
            vmem_limit_bytes=VMEM_LIMIT),
        name="hgrn2",
    )(hg, hg, hg, hg, lb_logits, gn)


def _ffn_kernel(x_ref, attn_ref, rec_ref, ga_ref, wout_ref, g2_ref, wup_ref,
                cw_ref, cb_ref, wdown_ref, gf_ref, o_ref,
                h_sc, u_sc, act_sc, carry_sc, nat_sc, *, tiles_per_seq):
    tm = x_ref.shape[0]
    i = pl.program_id(0)

    @pl.when(i % tiles_per_seq == 0)
    def _():
        carry_sc[...] = jnp.zeros_like(carry_sc)

    an = _rms(attn_ref[...], ga_ref[...])
    for r16 in range(ATTN_RESIDUES):
        for s in range(ATTN_WIDTH // LANES):
            nat_sc[s, pl.ds(r16, tm // ATTN_RESIDUES, stride=ATTN_RESIDUES), :] = (
                an[r16, :, s * LANES:(s + 1) * LANES])
    an = jnp.concatenate([nat_sc[s] for s in range(ATTN_WIDTH // LANES)], axis=1)
    mix = jnp.dot(an.astype(BF16), wout_ref[:ATTN_WIDTH, :], preferred_element_type=F32)
    mix = mix + jnp.dot(rec_ref[...], wout_ref[ATTN_WIDTH:, :],
                        preferred_element_type=F32)
    h = x_ref[...] + mix
    h_sc[...] = h
    u_sc[...] = _rms(h, g2_ref[...]).astype(BF16)

    row = lax.broadcasted_iota(jnp.int32, (tm, FF_CHUNK), 0)
    sqrt_half = math.sqrt(0.5)

    def chunk(c):
        u = u_sc[...]
        gate = jnp.dot(u, wup_ref[0, c], preferred_element_type=F32)
        val = jnp.dot(u, wup_ref[1, c], preferred_element_type=F32)
        prev = carry_sc[c]
        carry_sc[c] = gate[tm - SUBLANES:, :]
        p1 = jnp.broadcast_to(prev[SUBLANES - 1:SUBLANES], (tm, FF_CHUNK))
        p2 = jnp.broadcast_to(prev[SUBLANES - 2:SUBLANES - 1], (tm, FF_CHUNK))
        g1 = jnp.where(row == 0, p1, pltpu.roll(gate, 1, axis=0))
        g2 = jnp.where(row == 0, p2, jnp.where(row == 1, p1,
                                               pltpu.roll(gate, 2, axis=0)))
        cw = cw_ref[c]
        conv = cb_ref[c] + cw[0:1] * g2 + cw[1:2] * g1 + cw[2:3] * gate
        act = 0.5 * conv * (1.0 + lax.erf(conv * sqrt_half)) * val
        act_sc[:, c * FF_CHUNK:(c + 1) * FF_CHUNK] = act.astype(BF16)

    for c in range(N_FF_CHUNKS):
        chunk(c)
    down = jnp.dot(act_sc[...], wdown_ref[...], preferred_element_type=F32)
    o_ref[...] = _rms(h_sc[...] + down, gf_ref[...])


def _ffn(x2, attn, rec2, ga, wout, g2, wup, cw, cb, wdown, gf, tm, seq):
    T = x2.shape[0]
    tps = seq // tm
    const = lambda shape: pl.BlockSpec(shape, lambda i: (0,) * len(shape),
                                       pipeline_mode=pl.Buffered(1))
    return pl.pallas_call(
        functools.partial(_ffn_kernel, tiles_per_seq=tps),
        grid=(T // tm,),
        in_specs=[
            pl.BlockSpec((tm, D_MODEL), lambda i: (i, 0)),
            pl.BlockSpec((None, ATTN_RESIDUES, tm // ATTN_RESIDUES, ATTN_WIDTH),
                         lambda i: (i // tps, 0, i % tps, 0)),
            pl.BlockSpec((tm, HGRN_WIDTH), lambda i: (i, 0)),
            const((1, ATTN_WIDTH)),
            const((D_MODEL, D_MODEL)),
            const((1, D_MODEL)),
            const((2, N_FF_CHUNKS, D_MODEL, FF_CHUNK)),
            const((N_FF_CHUNKS, CONV_WIDTH, FF_CHUNK)),
            const((N_FF_CHUNKS, 1, FF_CHUNK)),
            const((D_FF, D_MODEL)),
            const((1, D_MODEL)),
        ],
        out_specs=pl.BlockSpec((tm, D_MODEL), lambda i: (i, 0)),
        out_shape=jax.ShapeDtypeStruct((T, D_MODEL), F32),
        scratch_shapes=[
            pltpu.VMEM((tm, D_MODEL), F32),
            pltpu.VMEM((tm, D_MODEL), BF16),
            pltpu.VMEM((tm, D_FF), BF16),
            pltpu.VMEM((N_FF_CHUNKS, SUBLANES, FF_CHUNK), F32),
            pltpu.VMEM((ATTN_WIDTH // LANES, tm, LANES), F32),
        ],
        compiler_params=pltpu.CompilerParams(
            dimension_semantics=("arbitrary",), vmem_limit_bytes=VMEM_LIMIT),
        name="outproj_convglu",
    )(x2, attn, rec2, ga, wout, g2, wup, cw, cb, wdown, gf)


def kernel(x, norm1_g, w_in, attn_norm_g, hgrn_norm_g, hgrn_lb_logits, w_out,
           norm2_g, w_up, conv_w, conv_b, w_down, final_norm_g):
    B, S, D = x.shape
    depth = w_in.shape[0]
    assert depth == 1 and D == D_MODEL
    assert S % (ATTN_BLOCK * DILATIONS[-1]) == 0 and S % (HGRN_ROWS * HGRN_GROUP) == 0
    tm = 512
    assert S % tm == 0
    T = B * S
    layer = 0
    x2 = x.reshape(T, D)

    att_ops, hg = _inproj(x2, norm1_g[layer].reshape(1, D), w_in[layer].astype(BF16), tm, S)

    attn = _attention(att_ops.reshape(B, S, ATT_OPS_WIDTH), _attn_bias())
    attn = attn.reshape(B, ATTN_RESIDUES, S // ATTN_RESIDUES, ATTN_WIDTH)

    lbl = hgrn_lb_logits.astype(F32).reshape(depth + 1, HGRN_HEADS, HGRN_DIM)
    lbl = lbl.transpose(1, 0, 2)
    gn = hgrn_norm_g[layer].reshape(HGRN_HEADS, 1, HGRN_DIM)
    rec = _hgrn(hg.reshape(B, S, HG_WIDTH), lbl, gn)

    wup = w_up[layer].astype(BF16).reshape(D, 2, N_FF_CHUNKS, FF_CHUNK).transpose(1, 2, 0, 3)
    cw = conv_w[layer].reshape(CONV_WIDTH, N_FF_CHUNKS, FF_CHUNK).transpose(1, 0, 2)
    cb = conv_b[layer].reshape(N_FF_CHUNKS, 1, FF_CHUNK)
    wdown = w_down[layer].astype(BF16)
    out = _ffn(x2, attn, rec.reshape(T, HGRN_WIDTH),
               attn_norm_g[layer].reshape(1, ATTN_WIDTH), w_out[layer].astype(BF16),
               norm2_g[layer].reshape(1, D), wup, cw, cb, wdown,
               final_norm_g.reshape(1, D), tm, S)
    return out.reshape(B, S, D)
```

```python
import functools
import math

import jax
import jax.numpy as jnp
from jax import lax
from jax.experimental import pallas as pl
from jax.experimental.pallas import tpu as pltpu

F32 = jnp.float32
BF16 = jnp.bfloat16

D_MODEL = 1024
ATTN_WIDTH = 512
ATTN_HEADS = 8
ATTN_HEAD_DIM = 64
ATTN_BLOCK = 128
DILATIONS = (1, 4, 16)
ATTN_RESIDUES = 16
HGRN_WIDTH = 512
HGRN_HEADS = 4
HGRN_DIM = 128
HGRN_CHUNK = 64
D_FF = 2816
CONV_WIDTH = 3
NORM_EPS = 1e-6
LOG2_E = math.log2(math.e)
QKV_WIDTH = 3 * ATTN_WIDTH
ATT_OPS_WIDTH = 5 * ATTN_WIDTH
HG_WIDTH = 4 * HGRN_WIDTH

LANES = 128
SUBLANES = 8
FF_CHUNK = 256
N_FF_CHUNKS = D_FF // FF_CHUNK
HGRN_ROWS = 256
HGRN_GROUP = 8
ATTN_GROUP = 16
VMEM_LIMIT = 56 * 1024 * 1024


def _rms(x, g):
    return x * lax.rsqrt(jnp.mean(x * x, axis=-1, keepdims=True) + NORM_EPS) * g


def _inproj_kernel(x_ref, g_ref, w_ref, att_ref, hg_ref, res_sc, tmp_sc):
    tm = x_ref.shape[0]
    run = tm // ATTN_RESIDUES
    quarter = tm // 4
    u = _rms(x_ref[...], g_ref[...]).astype(BF16)
    for c in range(QKV_WIDTH // FF_CHUNK):
        r = jnp.dot(u, w_ref[:, c * FF_CHUNK:(c + 1) * FF_CHUNK],
                    preferred_element_type=F32)
        res_sc[2 * c] = r[:, :LANES]
        res_sc[2 * c + 1] = r[:, LANES:]
    hg_ref[...] = jnp.dot(u, w_ref[:, QKV_WIDTH:],
                          preferred_element_type=F32).astype(hg_ref.dtype)
    head0 = lax.broadcasted_iota(jnp.int32, (run, LANES), 1) < ATTN_HEAD_DIM
    scale = ATTN_HEAD_DIM ** -0.5 * LOG2_E
    zero = jnp.zeros((run, LANES), F32)
    one = jnp.ones((run, LANES), F32)
    pairs = ATTN_WIDTH // LANES
    for s in range(QKV_WIDTH // LANES):
        kind, p = divmod(s, pairs)
        cols = lambda group: slice((group * pairs + p) * LANES, (group * pairs + p + 1) * LANES)
        for r4 in range(4):
            tmp_sc[s, r4 * quarter:(r4 + 1) * quarter, :] = res_sc[s, pl.ds(r4, quarter, stride=4), :]
        for r16 in range(ATTN_RESIDUES):
            rows = tmp_sc[s, pl.ds((r16 % 4) * quarter + r16 // 4, run, stride=4), :]
            if kind == 0:
                q = rows * scale
                att_ref[r16, :, cols(0)] = jnp.where(head0, q, zero).astype(BF16)
                att_ref[r16, :, cols(1)] = jnp.where(head0, zero, q).astype(BF16)
            elif kind == 1:
                att_ref[r16, :, cols(2)] = rows.astype(BF16)
            else:
                att_ref[r16, :, cols(3)] = jnp.where(head0, rows, one).astype(BF16)
                att_ref[r16, :, cols(4)] = jnp.where(head0, one, rows).astype(BF16)


def _inproj(x2, g, w, tm, seq):
    T = x2.shape[0]
    tps = seq // tm
    run = tm // ATTN_RESIDUES
    return pl.pallas_call(
        _inproj_kernel,
        grid=(T // tm,),
        in_specs=[
            pl.BlockSpec((tm, D_MODEL), lambda i: (i, 0)),
            pl.BlockSpec((1, D_MODEL), lambda i: (0, 0)),
            pl.BlockSpec((D_MODEL, QKV_WIDTH + HG_WIDTH), lambda i: (0, 0),
                         pipeline_mode=pl.Buffered(1)),
        ],
        out_specs=[
            pl.BlockSpec((None, ATTN_RESIDUES, run, ATT_OPS_WIDTH),
                         lambda i: (i // tps, 0, i % tps, 0)),
            pl.BlockSpec((tm, HG_WIDTH), lambda i: (i, 0)),
        ],
        out_shape=[
            jax.ShapeDtypeStruct((T // seq, ATTN_RESIDUES, seq // ATTN_RESIDUES,
                                  ATT_OPS_WIDTH), BF16),
            jax.ShapeDtypeStruct((T, HG_WIDTH), BF16),
        ],
        scratch_shapes=[pltpu.VMEM((QKV_WIDTH // LANES, tm, LANES), F32)] * 2,
        compiler_params=pltpu.CompilerParams(
            dimension_semantics=("arbitrary",), vmem_limit_bytes=VMEM_LIMIT),
        name="inproj",
    )(x2, g, w)


def _attn_runs(d):
    g = ATTN_RESIDUES // d
    return g, ATTN_BLOCK // g


def _attn_bias():
    blk = ATTN_BLOCK
    e = jnp.arange(2 * blk) % blk
    half = jnp.arange(2 * blk) // blk
    out = []
    for d in DILATIONS:
        g, run = _attn_runs(d)
        pos = g * (e % run) + e // run
        pos_q = pos[:, None]
        pos_k = (blk * half + pos)[None, :]
        dist = blk + pos_q - pos_k
        out.append((dist >= 0) & (dist <= blk))
        out.append(pos_k <= pos_q)
    return jnp.where(jnp.stack(out), 0.0, -jnp.inf).astype(F32)


def _attn_kernel(q0_ref, q1_ref, k_ref, v0_ref, v1_ref, bias_ref, o_ref,
                 m_sc, l_sc, acc_sc, *, seq):
    blk = ATTN_BLOCK
    n_blocks = seq // blk
    ns = seq // ATTN_RESIDUES
    head0 = lax.broadcasted_iota(jnp.int32, (blk, LANES), 1) < ATTN_HEAD_DIM

    def gather(ref, runs):
        return jnp.concatenate([ref[rs, :] for rs in runs], axis=0)

    def block_runs(d, j, which):
        nb = n_blocks // d
        g, run = _attn_runs(d)
        r, n = divmod(j, nb)
        n0 = max(n - 1, 0)
        nsel = {"q": [n], "kv": [n0, n0 + 1]}[which]
        return [pl.ds((d * jj + r) * ns + run * nn, run) for nn in nsel for jj in range(g)]

    def scores_of(group):
        out = []
        for d, j in group:
            qruns = block_runs(d, j, "q")
            qs = jnp.concatenate([gather(q0_ref, qruns),
                                  gather(q1_ref, qruns)], axis=0)
            kw = gather(k_ref, block_runs(d, j, "kv"))
            out.append(lax.dot_general(qs, kw, (((1,), (1,)), ((), ())),
                                       preferred_element_type=F32))
        return out

    def finish(group, scores):
        parts = []
        for (d, j), s in zip(group, scores):
            kruns = block_runs(d, j, "kv")
            first = 1 if j % (n_blocks // d) == 0 else 0
            s = s + bias_ref[2 * DILATIONS.index(d) + first]
            m = jnp.max(s, axis=-1, keepdims=True)
            p = jnp.exp2(s - m).astype(BF16)
            pv0 = jnp.dot(p[:blk], gather(v0_ref, kruns),
                          preferred_element_type=F32)
            pv1 = jnp.dot(p[blk:], gather(v1_ref, kruns),
                          preferred_element_type=F32)
            parts.append((pv0, pv1, m))
        for (d, j), (pv0, pv1, m) in zip(group, parts):
            bi = DILATIONS.index(d)
            qruns = block_runs(d, j, "q")
            o_b = jnp.where(head0, pv0, pv1)
            l_b = jnp.where(head0, pv1, pv0)
            m_b = jnp.where(head0, jnp.broadcast_to(m[:blk], (blk, LANES)),
                            jnp.broadcast_to(m[blk:], (blk, LANES)))
            run_len = blk // len(qruns)
            for i, rs in enumerate(qruns):
                piece = slice(i * run_len, (i + 1) * run_len)
                acc_sc[bi, rs, :] = o_b[piece]
                m_sc[bi, rs, :] = m_b[piece]
                l_sc[bi, rs, :] = l_b[piece]

    blocks = [(d, j) for d in DILATIONS for j in range(n_blocks)]
    groups = [blocks[i:i + ATTN_GROUP] for i in range(0, len(blocks), ATTN_GROUP)]
    scores = scores_of(groups[0])
    for i, group in enumerate(groups):
        nxt = scores_of(groups[i + 1]) if i + 1 < len(groups) else None
        finish(group, scores)
        scores = nxt

    def merge_body(t, carry):
        rows = pl.ds(pl.multiple_of(t * blk, blk), blk)
        ms = [m_sc[bi, rows, :] for bi in range(len(DILATIONS))]
        m_all = functools.reduce(jnp.maximum, ms)
        den = jnp.zeros((blk, LANES), F32)
        num = jnp.zeros((blk, LANES), F32)
        for bi in range(len(DILATIONS)):
            w = jnp.exp2(ms[bi] - m_all)
            den = den + w * pltpu.roll(l_sc[bi, rows, :], ATTN_HEAD_DIM, axis=1)
            num = num + w * acc_sc[bi, rows, :]
        o_ref[rows, :] = num / den
        return carry

    lax.fori_loop(0, n_blocks, merge_body, 0, unroll=4)


def _attention(att_ops, bias):
    B, S, _ = att_ops.shape
    n_pairs = ATTN_WIDTH // LANES
    blockspec = lambda group: pl.BlockSpec((None, S, LANES),
                                           lambda b, p: (b, 0, group * n_pairs + p))
    return pl.pallas_call(
        functools.partial(_attn_kernel, seq=S),
        grid=(B, n_pairs),
        in_specs=[blockspec(group) for group in range(ATT_OPS_WIDTH // ATTN_WIDTH)] + [
            pl.BlockSpec(bias.shape, lambda b, p: (0, 0, 0), pipeline_mode=pl.Buffered(1)),
        ],
        out_specs=pl.BlockSpec((None, S, LANES), lambda b, p: (b, 0, p)),
        out_shape=jax.ShapeDtypeStruct((B, S, ATTN_WIDTH), F32),
        scratch_shapes=[pltpu.VMEM((len(DILATIONS), S, LANES), F32)] * 3,
        compiler_params=pltpu.CompilerParams(
            dimension_semantics=("arbitrary", "arbitrary"),
            vmem_limit_bytes=VMEM_LIMIT),
        name="dilated_attn",
    )(*([att_ops] * (ATT_OPS_WIDTH // ATTN_WIDTH)), bias)


def _hgrn_kernel(q_ref, f_ref, i_ref, g_ref, lbl_ref, gn_ref, o_ref, st_sc, tri_sc, *, seq):
    R = HGRN_ROWS
    C = HGRN_CHUNK
    logits = lbl_ref[...]
    e = jnp.exp(logits - jnp.max(logits, axis=0, keepdims=True))
    lb = e[0:1] / jnp.sum(e, axis=0, keepdims=True)
    f_mid = 0.5 * (1.0 + lb)
    f_half = 0.5 * (1.0 - lb)
    gn = gn_ref[...]

    st_sc[...] = jnp.zeros_like(st_sc)
    row = lax.broadcasted_iota(jnp.int32, (R, R), 0)
    col = lax.broadcasted_iota(jnp.int32, (R, R), 1)
    causal = (row >= col) & ((row // C) == (col // C))
    tri_sc[...] = jnp.where(causal, 1.0, 0.0).astype(BF16)

    G = HGRN_GROUP
    NC = R // C

    def body(it, carry):
        rows = [pl.ds(pl.multiple_of((it * G + g) * R, R), R) for g in range(G)]
        ivs, keys, qfs, b2s = [], [], [], []
        for g in range(G):
            q = q_ref[rows[g], :].astype(F32)
            f = f_ref[rows[g], :].astype(F32)
            ivs.append(i_ref[rows[g], :].astype(BF16))
            hq = 0.5 * q
            qfs.append(hq + hq * jnp.tanh(hq))
            forget = f_mid + f_half * jnp.tanh(0.5 * f)
            keys.append(1.0 - forget)
            lf = jnp.log2(forget)
            lf_hi = lf.astype(BF16)
            lf_lo = (lf - lf_hi.astype(F32)).astype(BF16)
            b2s.append(jnp.dot(tri_sc[...], jnp.concatenate([lf_hi, lf_lo], axis=1),
                               preferred_element_type=F32))
        q_decs, k_ends, decays, a_s = [], [], [], []
        for g in range(G):
            b = b2s[g][:, :HGRN_DIM] + b2s[g][:, HGRN_DIM:]
            decay_rows = [jnp.exp2(b[j * C + C - 1:j * C + C]) for j in range(NC)]
            decay_end = jnp.concatenate(
                [jnp.broadcast_to(d, (C, HGRN_DIM)) for d in decay_rows], axis=0)
            k_inv_f = keys[g] * jnp.exp2(-b)
            q_dec = (qfs[g] * jnp.exp2(b)).astype(BF16)
            k_ends.append((k_inv_f * decay_end).astype(BF16))
            q_decs.append(q_dec)
            decays.append(decay_rows)
            a_s.append(lax.dot_general(q_dec, k_inv_f.astype(BF16),
                                       (((1,), (1,)), ((), ())),
                                       preferred_element_type=F32))
        o_intras, u_ts = [], []
        for g in range(G):
            a = jnp.where(causal, a_s[g], 0.0).astype(BF16)
            o_intras.append(jnp.dot(a, ivs[g], preferred_element_type=F32))
            u_ts.append([lax.dot_general(ivs[g][j * C:(j + 1) * C],
                                         k_ends[g][j * C:(j + 1) * C],
                                         (((0,), (0,)), ((), ())),
                                         preferred_element_type=F32)
                         for j in range(NC)])
        st = st_sc[...]
        o_inters = []
        for g in range(G):
            parts = []
            for j in range(NC):
                parts.append(lax.dot_general(q_decs[g][j * C:(j + 1) * C], st.astype(BF16),
                                             (((1,), (1,)), ((), ())),
                                             preferred_element_type=F32))
                st = st * decays[g][j] + u_ts[g][j]
            o_inters.append(jnp.concatenate(parts, axis=0))
        st_sc[...] = st
        for g in range(G):
            o = _rms(o_intras[g] + o_inters[g], gn)
            hg = 0.5 * g_ref[rows[g], :].astype(F32)
            o_ref[rows[g], :] = (o * (hg + hg * jnp.tanh(hg))).astype(o_ref.dtype)
        return carry

    lax.fori_loop(0, seq // (R * G), body, 0)


def _hgrn(hg, lb_logits, gn):
    B, S, _ = hg.shape
    H = HGRN_HEADS
    n_rows = lb_logits.shape[1]
    blockspec = lambda off: pl.BlockSpec((None, S, HGRN_DIM), lambda b, h: (b, 0, h + off))
    return pl.pallas_call(
        functools.partial(_hgrn_kernel, seq=S),
        grid=(B, H),
        in_specs=[
            blockspec(0), blockspec(H), blockspec(2 * H), blockspec(3 * H),
            pl.BlockSpec((None, n_rows, HGRN_DIM), lambda b, h: (h, 0, 0)),
            pl.BlockSpec((None, 1, HGRN_DIM), lambda b, h: (h, 0, 0)),
        ],
        out_specs=pl.BlockSpec((None, S, HGRN_DIM), lambda b, h: (b, 0, h)),
        out_shape=jax.ShapeDtypeStruct((B, S, HGRN_WIDTH), BF16),
        scratch_shapes=[pltpu.VMEM((HGRN_DIM, HGRN_DIM), F32),
                        pltpu.VMEM((HGRN_ROWS, HGRN_ROWS), BF16)],
        compiler_params=pltpu.CompilerParams(
            dimension_semantics=("arbitrary", "arbitrary"),
            vmem_limit_bytes=VMEM_LIMIT),
        name="hgrn2",
    )(hg, hg, hg, hg, lb_logits, gn)


def _ffn_kernel(x_ref, attn_ref, rec_ref, ga_ref, wout_ref, g2_ref, wup_ref,
                cw_ref, cb_ref, wdown_ref, gf_ref, o_ref,
                h_sc, u_sc, act_sc, carry_sc, nat_sc, *, tiles_per_seq):
    tm = x_ref.shape[0]
    i = pl.program_id(0)

    @pl.when(i % tiles_per_seq == 0)
    def _():
        carry_sc[...] = jnp.zeros_like(carry_sc)

    an = _rms(attn_ref[...], ga_ref[...])
    for r16 in range(ATTN_RESIDUES):
        for s in range(ATTN_WIDTH // LANES):
            nat_sc[s, pl.ds(r16, tm // ATTN_RESIDUES, stride=ATTN_RESIDUES), :] = (
                an[r16, :, s * LANES:(s + 1) * LANES])
    an = jnp.concatenate([nat_sc[s] for s in range(ATTN_WIDTH // LANES)], axis=1)
    mix = jnp.dot(an.astype(BF16), wout_ref[:ATTN_WIDTH, :], preferred_element_type=F32)
    mix = mix + jnp.dot(rec_ref[...], wout_ref[ATTN_WIDTH:, :],
                        preferred_element_type=F32)
    h = x_ref[...] + mix
    h_sc[...] = h
    u_sc[...] = _rms(h, g2_ref[...]).astype(BF16)

    row = lax.broadcasted_iota(jnp.int32, (tm, FF_CHUNK), 0)
    sqrt_half = math.sqrt(0.5)

    def chunk(c):
        u = u_sc[...]
        cols = slice(c * FF_CHUNK, (c + 1) * FF_CHUNK)
        vcols = slice(D_FF + c * FF_CHUNK, D_FF + (c + 1) * FF_CHUNK)
        gate = jnp.dot(u, wup_ref[:, cols], preferred_element_type=F32)
        val = jnp.dot(u, wup_ref[:, vcols], preferred_element_type=F32)
        prev = carry_sc[c]
        carry_sc[c] = gate[tm - SUBLANES:, :]
        p1 = jnp.broadcast_to(prev[SUBLANES - 1:SUBLANES], (tm, FF_CHUNK))
        p2 = jnp.broadcast_to(prev[SUBLANES - 2:SUBLANES - 1], (tm, FF_CHUNK))
        g1 = jnp.where(row == 0, p1, pltpu.roll(gate, 1, axis=0))
        g2 = jnp.where(row == 0, p2, jnp.where(row == 1, p1,
                                               pltpu.roll(gate, 2, axis=0)))
        cw = cw_ref[:, cols]
        conv = cb_ref[:, cols] + cw[0:1] * g2 + cw[1:2] * g1 + cw[2:3] * gate
        act = 0.5 * conv * (1.0 + lax.erf(conv * sqrt_half)) * val
        act_sc[:, cols] = act.astype(BF16)

    for c in range(N_FF_CHUNKS):
        chunk(c)
    down = jnp.dot(act_sc[...], wdown_ref[...], preferred_element_type=F32)
    o_ref[...] = _rms(h_sc[...] + down, gf_ref[...])


def _ffn(x2, attn, rec2, ga, wout, g2, wup, cw, cb, wdown, gf, tm, seq):
    T = x2.shape[0]
    tps = seq // tm
    const = lambda shape: pl.BlockSpec(shape, lambda i: (0,) * len(shape),
                                       pipeline_mode=pl.Buffered(1))
    return pl.pallas_call(
        functools.partial(_ffn_kernel, tiles_per_seq=tps),
        grid=(T // tm,),
        in_specs=[
            pl.BlockSpec((tm, D_MODEL), lambda i: (i, 0)),
            pl.BlockSpec((None, ATTN_RESIDUES, tm // ATTN_RESIDUES, ATTN_WIDTH),
                         lambda i: (i // tps, 0, i % tps, 0)),
            pl.BlockSpec((tm, HGRN_WIDTH), lambda i: (i, 0)),
            const((1, ATTN_WIDTH)),
            const((D_MODEL, D_MODEL)),
            const((1, D_MODEL)),
            const((D_MODEL, 2 * D_FF)),
            const((CONV_WIDTH, D_FF)),
            const((1, D_FF)),
            const((D_FF, D_MODEL)),
            const((1, D_MODEL)),
        ],
        out_specs=pl.BlockSpec((tm, D_MODEL), lambda i: (i, 0)),
        out_shape=jax.ShapeDtypeStruct((T, D_MODEL), F32),
        scratch_shapes=[
            pltpu.VMEM((tm, D_MODEL), F32),
            pltpu.VMEM((tm, D_MODEL), BF16),
            pltpu.VMEM((tm, D_FF), BF16),
            pltpu.VMEM((N_FF_CHUNKS, SUBLANES, FF_CHUNK), F32),
            pltpu.VMEM((ATTN_WIDTH // LANES, tm, LANES), F32),
        ],
        compiler_params=pltpu.CompilerParams(
            dimension_semantics=("arbitrary",), vmem_limit_bytes=VMEM_LIMIT),
        name="outproj_convglu",
    )(x2, attn, rec2, ga, wout, g2, wup, cw, cb, wdown, gf)


def kernel(x, norm1_g, w_in, attn_norm_g, hgrn_norm_g, hgrn_lb_logits, w_out,
           norm2_g, w_up, conv_w, conv_b, w_down, final_norm_g):
    B, S, D = x.shape
    depth = w_in.shape[0]
    assert depth == 1 and D == D_MODEL
    assert S % (ATTN_BLOCK * DILATIONS[-1]) == 0 and S % (HGRN_ROWS * HGRN_GROUP) == 0
    tm = 512
    assert S % tm == 0
    T = B * S
    layer = 0
    x2 = x.reshape(T, D)

    att_ops, hg = _inproj(x2, norm1_g[layer].reshape(1, D), w_in[layer].astype(BF16), tm, S)

    attn = _attention(att_ops.reshape(B, S, ATT_OPS_WIDTH), _attn_bias())
    attn = attn.reshape(B, ATTN_RESIDUES, S // ATTN_RESIDUES, ATTN_WIDTH)

    lbl = hgrn_lb_logits.astype(F32).reshape(depth + 1, HGRN_HEADS, HGRN_DIM)
    lbl = lbl.transpose(1, 0, 2)
    gn = hgrn_norm_g[layer].reshape(HGRN_HEADS, 1, HGRN_DIM)
    rec = _hgrn(hg.reshape(B, S, HG_WIDTH), lbl, gn)

    wup = w_up[layer].astype(BF16)
    cw = conv_w[layer]
    cb = conv_b[layer].reshape(1, D_FF)
    wdown = w_down[layer].astype(BF16)
    out = _ffn(x2, attn, rec.reshape(T, HGRN_WIDTH),
               attn_norm_g[layer].reshape(1, ATTN_WIDTH), w_out[layer].astype(BF16),
               norm2_g[layer].reshape(1, D), wup, cw, cb, wdown,
               final_norm_g.reshape(1, D), tm, S)
    return out.reshape(B, S, D)
```

```python
import functools
import math

import jax
import jax.numpy as jnp
from jax import lax
from jax.experimental import pallas as pl
from jax.experimental.pallas import tpu as pltpu

F32 = jnp.float32
BF16 = jnp.bfloat16

D_MODEL = 1024
ATTN_WIDTH = 512
ATTN_HEADS = 8
ATTN_HEAD_DIM = 64
ATTN_BLOCK = 128
DILATIONS = (1, 4, 16)
ATTN_RESIDUES = 16
HGRN_WIDTH = 512
HGRN_HEADS = 4
HGRN_DIM = 128
HGRN_CHUNK = 64
D_FF = 2816
CONV_WIDTH = 3
NORM_EPS = 1e-6
LOG2_E = math.log2(math.e)
QKV_WIDTH = 3 * ATTN_WIDTH
ATT_OPS_WIDTH = 5 * ATTN_WIDTH
HG_WIDTH = 4 * HGRN_WIDTH

LANES = 128
SUBLANES = 8
FF_CHUNK = 256
N_FF_CHUNKS = D_FF // FF_CHUNK
HGRN_ROWS = 256
HGRN_GROUP = 16
ATTN_GROUP = 16
VMEM_LIMIT = 56 * 1024 * 1024


def _rms(x, g):
    return x * lax.rsqrt(jnp.mean(x * x, axis=-1, keepdims=True) + NORM_EPS) * g


def _inproj_kernel(x_ref, g_ref, w_ref, att_ref, hg_ref, res_sc, tmp_sc):
    tm = x_ref.shape[0]
    run = tm // ATTN_RESIDUES
    quarter = tm // 4
    u = _rms(x_ref[...], g_ref[...]).astype(BF16)
    for c in range(QKV_WIDTH // FF_CHUNK):
        r = jnp.dot(u, w_ref[:, c * FF_CHUNK:(c + 1) * FF_CHUNK],
                    preferred_element_type=F32)
        res_sc[2 * c] = r[:, :LANES]
        res_sc[2 * c + 1] = r[:, LANES:]
    hg_ref[...] = jnp.dot(u, w_ref[:, QKV_WIDTH:],
                          preferred_element_type=F32).astype(hg_ref.dtype)
    head0 = lax.broadcasted_iota(jnp.int32, (run, LANES), 1) < ATTN_HEAD_DIM
    scale = ATTN_HEAD_DIM ** -0.5 * LOG2_E
    zero = jnp.zeros((run, LANES), F32)
    one = jnp.ones((run, LANES), F32)
    pairs = ATTN_WIDTH // LANES
    for s in range(QKV_WIDTH // LANES):
        kind, p = divmod(s, pairs)
        cols = lambda group: slice((group * pairs + p) * LANES, (group * pairs + p + 1) * LANES)
        for r4 in range(4):
            tmp_sc[s, r4 * quarter:(r4 + 1) * quarter, :] = res_sc[s, pl.ds(r4, quarter, stride=4), :]
        for r16 in range(ATTN_RESIDUES):
            rows = tmp_sc[s, pl.ds((r16 % 4) * quarter + r16 // 4, run, stride=4), :]
            if kind == 0:
                q = rows * scale
                att_ref[r16, :, cols(0)] = jnp.where(head0, q, zero).astype(BF16)
                att_ref[r16, :, cols(1)] = jnp.where(head0, zero, q).astype(BF16)
            elif kind == 1:
                att_ref[r16, :, cols(2)] = rows.astype(BF16)
            else:
                att_ref[r16, :, cols(3)] = jnp.where(head0, rows, one).astype(BF16)
                att_ref[r16, :, cols(4)] = jnp.where(head0, one, rows).astype(BF16)


def _inproj(x2, g, w, tm, seq):
    T = x2.shape[0]
    tps = seq // tm
    run = tm // ATTN_RESIDUES
    return pl.pallas_call(
        _inproj_kernel,
        grid=(T // tm,),
        in_specs=[
            pl.BlockSpec((tm, D_MODEL), lambda i: (i, 0)),
            pl.BlockSpec((1, D_MODEL), lambda i: (0, 0)),
            pl.BlockSpec((D_MODEL, QKV_WIDTH + HG_WIDTH), lambda i: (0, 0),
                         pipeline_mode=pl.Buffered(1)),
        ],
        out_specs=[
            pl.BlockSpec((None, ATTN_RESIDUES, run, ATT_OPS_WIDTH),
                         lambda i: (i // tps, 0, i % tps, 0)),
            pl.BlockSpec((tm, HG_WIDTH), lambda i: (i, 0)),
        ],
        out_shape=[
            jax.ShapeDtypeStruct((T // seq, ATTN_RESIDUES, seq // ATTN_RESIDUES,
                                  ATT_OPS_WIDTH), BF16),
            jax.ShapeDtypeStruct((T, HG_WIDTH), BF16),
        ],
        scratch_shapes=[pltpu.VMEM((QKV_WIDTH // LANES, tm, LANES), F32)] * 2,
        compiler_params=pltpu.CompilerParams(
            dimension_semantics=("arbitrary",), vmem_limit_bytes=VMEM_LIMIT),
        name="inproj",
    )(x2, g, w)


def _attn_runs(d):
    g = ATTN_RESIDUES // d
    return g, ATTN_BLOCK // g


def _attn_bias():
    blk = ATTN_BLOCK
    e = jnp.arange(2 * blk) % blk
    half = jnp.arange(2 * blk) // blk
    out = []
    for d in DILATIONS:
        g, run = _attn_runs(d)
        pos = g * (e % run) + e // run
        pos_q = pos[:, None]
        pos_k = (blk * half + pos)[None, :]
        dist = blk + pos_q - pos_k
        out.append((dist >= 0) & (dist <= blk))
        out.append(pos_k <= pos_q)
    return jnp.where(jnp.stack(out), 0.0, -jnp.inf).astype(F32)


def _attn_kernel(q0_ref, q1_ref, k_ref, v0_ref, v1_ref, bias_ref, o_ref,
                 m_sc, l_sc, acc_sc, *, seq):
    blk = ATTN_BLOCK
    n_blocks = seq // blk
    ns = seq // ATTN_RESIDUES
    head0 = lax.broadcasted_iota(jnp.int32, (blk, LANES), 1) < ATTN_HEAD_DIM

    def gather(ref, runs):
        return jnp.concatenate([ref[rs, :] for rs in runs], axis=0)

    def block_runs(d, j, which):
        nb = n_blocks // d
        g, run = _attn_runs(d)
        r, n = divmod(j, nb)
        n0 = max(n - 1, 0)
        nsel = {"q": [n], "kv": [n0, n0 + 1]}[which]
        return [pl.ds((d * jj + r) * ns + run * nn, run) for nn in nsel for jj in range(g)]

    def scores_of(group):
        out = []
        for d, j in group:
            qruns = block_runs(d, j, "q")
            qs = jnp.concatenate([gather(q0_ref, qruns),
                                  gather(q1_ref, qruns)], axis=0)
            kw = gather(k_ref, block_runs(d, j, "kv"))
            out.append(lax.dot_general(qs, kw, (((1,), (1,)), ((), ())),
                                       preferred_element_type=F32))
        return out

    def finish(group, scores):
        parts = []
        for (d, j), s in zip(group, scores):
            kruns = block_runs(d, j, "kv")
            first = 1 if j % (n_blocks // d) == 0 else 0
            s = s + bias_ref[2 * DILATIONS.index(d) + first]
            m = jnp.max(s, axis=-1, keepdims=True)
            p = jnp.exp2(s - m).astype(BF16)
            pv0 = jnp.dot(p[:blk], gather(v0_ref, kruns),
                          preferred_element_type=F32)
            pv1 = jnp.dot(p[blk:], gather(v1_ref, kruns),
                          preferred_element_type=F32)
            parts.append((pv0, pv1, m))
        for (d, j), (pv0, pv1, m) in zip(group, parts):
            bi = DILATIONS.index(d)
            qruns = block_runs(d, j, "q")
            o_b = jnp.where(head0, pv0, pv1)
            l_b = jnp.where(head0, pv1, pv0)
            m_b = jnp.where(head0, jnp.broadcast_to(m[:blk], (blk, LANES)),
                            jnp.broadcast_to(m[blk:], (blk, LANES)))
            run_len = blk // len(qruns)
            for i, rs in enumerate(qruns):
                piece = slice(i * run_len, (i + 1) * run_len)
                acc_sc[bi, rs, :] = o_b[piece]
                m_sc[bi, rs, :] = m_b[piece]
                l_sc[bi, rs, :] = l_b[piece]

    blocks = [(d, j) for d in DILATIONS for j in range(n_blocks)]
    groups = [blocks[i:i + ATTN_GROUP] for i in range(0, len(blocks), ATTN_GROUP)]
    scores = scores_of(groups[0])
    for i, group in enumerate(groups):
        nxt = scores_of(groups[i + 1]) if i + 1 < len(groups) else None
        finish(group, scores)
        scores = nxt

    def merge_body(t, carry):
        rows = pl.ds(pl.multiple_of(t * blk, blk), blk)
        ms = [m_sc[bi, rows, :] for bi in range(len(DILATIONS))]
        m_all = functools.reduce(jnp.maximum, ms)
        den = jnp.zeros((blk, LANES), F32)
        num = jnp.zeros((blk, LANES), F32)
        for bi in range(len(DILATIONS)):
            w = jnp.exp2(ms[bi] - m_all)
            den = den + w * pltpu.roll(l_sc[bi, rows, :], ATTN_HEAD_DIM, axis=1)
            num = num + w * acc_sc[bi, rows, :]
        o_ref[rows, :] = num / den
        return carry

    lax.fori_loop(0, n_blocks, merge_body, 0, unroll=4)


def _attention(att_ops, bias):
    B, S, _ = att_ops.shape
    n_pairs = ATTN_WIDTH // LANES
    blockspec = lambda group: pl.BlockSpec((None, S, LANES),
                                           lambda b, p: (b, 0, group * n_pairs + p))
    return pl.pallas_call(
        functools.partial(_attn_kernel, seq=S),
        grid=(B, n_pairs),
        in_specs=[blockspec(group) for group in range(ATT_OPS_WIDTH // ATTN_WIDTH)] + [
            pl.BlockSpec(bias.shape, lambda b, p: (0, 0, 0), pipeline_mode=pl.Buffered(1)),
        ],
        out_specs=pl.BlockSpec((None, S, LANES), lambda b, p: (b, 0, p)),
        out_shape=jax.ShapeDtypeStruct((B, S, ATTN_WIDTH), F32),
        scratch_shapes=[pltpu.VMEM((len(DILATIONS), S, LANES), F32)] * 3,
        compiler_params=pltpu.CompilerParams(
            dimension_semantics=("arbitrary", "arbitrary"),
            vmem_limit_bytes=VMEM_LIMIT),
        name="dilated_attn",
    )(*([att_ops] * (ATT_OPS_WIDTH // ATTN_WIDTH)), bias)


def _hgrn_kernel(q_ref, f_ref, i_ref, g_ref, lbl_ref, gn_ref, o_ref, st_sc, tri_sc, *, seq):
    R = HGRN_ROWS
    C = HGRN_CHUNK
    logits = lbl_ref[...]
    e = jnp.exp(logits - jnp.max(logits, axis=0, keepdims=True))
    lb = e[0:1] / jnp.sum(e, axis=0, keepdims=True)
    f_mid = 0.5 * (1.0 + lb)
    f_half = 0.5 * (1.0 - lb)
    gn = gn_ref[...]

    st_sc[...] = jnp.zeros_like(st_sc)
    row = lax.broadcasted_iota(jnp.int32, (R, R), 0)
    col = lax.broadcasted_iota(jnp.int32, (R, R), 1)
    causal = (row >= col) & ((row // C) == (col // C))
    tri_sc[...] = jnp.where(causal, 1.0, 0.0).astype(BF16)

    G = HGRN_GROUP
    NC = R // C

    def body(it, carry):
        rows = [pl.ds(pl.multiple_of((it * G + g) * R, R), R) for g in range(G)]
        ivs, keys, qfs, b2s = {}, {}, {}, {}
        q_decs, k_ends, decays, a_s = {}, {}, {}, {}
        o_intras, u_ts, o_inters = {}, {}, {}
        st = [st_sc[...]]

        def gates_and_prefix_sum(g):
            q = q_ref[rows[g], :].astype(F32)
            f = f_ref[rows[g], :].astype(F32)
            ivs[g] = i_ref[rows[g], :].astype(BF16)
            hq = 0.5 * q
            qfs[g] = hq + hq * jnp.tanh(hq)
            forget = f_mid + f_half * jnp.tanh(0.5 * f)
            keys[g] = 1.0 - forget
            lf = jnp.log2(forget)
            lf_hi = lf.astype(BF16)
            lf_lo = (lf - lf_hi.astype(F32)).astype(BF16)
            b2s[g] = jnp.dot(tri_sc[...], jnp.concatenate([lf_hi, lf_lo], axis=1),
                             preferred_element_type=F32)

        def intra_chunk_scores(g):
            b2 = b2s.pop(g)
            b = b2[:, :HGRN_DIM] + b2[:, HGRN_DIM:]
            decay_rows = [jnp.exp2(b[j * C + C - 1:j * C + C]) for j in range(NC)]
            decay_end = jnp.concatenate(
                [jnp.broadcast_to(d, (C, HGRN_DIM)) for d in decay_rows], axis=0)
            k_inv_f = keys.pop(g) * jnp.exp2(-b)
            q_decs[g] = (qfs.pop(g) * jnp.exp2(b)).astype(BF16)
            k_ends[g] = (k_inv_f * decay_end).astype(BF16)
            decays[g] = decay_rows
            a_s[g] = lax.dot_general(q_decs[g], k_inv_f.astype(BF16),
                                     (((1,), (1,)), ((), ())),
                                     preferred_element_type=F32)

        def intra_chunk_outputs(g):
            a = jnp.where(causal, a_s.pop(g), 0.0).astype(BF16)
            iv = ivs.pop(g)
            k_end = k_ends.pop(g)
            o_intras[g] = jnp.dot(a, iv, preferred_element_type=F32)
            u_ts[g] = [lax.dot_general(iv[j * C:(j + 1) * C], k_end[j * C:(j + 1) * C],
                                       (((0,), (0,)), ((), ())),
                                       preferred_element_type=F32)
                       for j in range(NC)]

        def state_scan(g):
            parts = []
            q_dec = q_decs.pop(g)
            for j in range(NC):
                parts.append(lax.dot_general(q_dec[j * C:(j + 1) * C], st[0].astype(BF16),
                                             (((1,), (1,)), ((), ())),
                                             preferred_element_type=F32))
                st[0] = st[0] * decays[g][j] + u_ts[g][j]
            o_inters[g] = jnp.concatenate(parts, axis=0)

        def emit(g):
            o = _rms(o_intras.pop(g) + o_inters.pop(g), gn)
            hg = 0.5 * g_ref[rows[g], :].astype(F32)
            o_ref[rows[g], :] = (o * (hg + hg * jnp.tanh(hg))).astype(o_ref.dtype)

        stages = (gates_and_prefix_sum, intra_chunk_scores, intra_chunk_outputs,
                  state_scan, emit)
        for t in range(G + len(stages) - 1):
            for k in reversed(range(len(stages))):
                g = t - k
                if 0 <= g < G:
                    stages[k](g)
        st_sc[...] = st[0]
        return carry

    lax.fori_loop(0, seq // (R * G), body, 0)


def _hgrn(hg, lb_logits, gn):
    B, S, _ = hg.shape
    H = HGRN_HEADS
    n_rows = lb_logits.shape[1]
    blockspec = lambda off: pl.BlockSpec((None, S, HGRN_DIM), lambda b, h: (b, 0, h + off))
    return pl.pallas_call(
        functools.partial(_hgrn_kernel, seq=S),
        grid=(B, H),
        in_specs=[
            blockspec(0), blockspec(H), blockspec(2 * H), blockspec(3 * H),
            pl.BlockSpec((None, n_rows, HGRN_DIM), lambda b, h: (h, 0, 0)),
            pl.BlockSpec((None, 1, HGRN_DIM), lambda b, h: (h, 0, 0)),
        ],
        out_specs=pl.BlockSpec((None, S, HGRN_DIM), lambda b, h: (b, 0, h)),
        out_shape=jax.ShapeDtypeStruct((B, S, HGRN_WIDTH), BF16),
        scratch_shapes=[pltpu.VMEM((HGRN_DIM, HGRN_DIM), F32),
                        pltpu.VMEM((HGRN_ROWS, HGRN_ROWS), BF16)],
        compiler_params=pltpu.CompilerParams(
            dimension_semantics=("arbitrary", "arbitrary"),
            vmem_limit_bytes=VMEM_LIMIT),
        name="hgrn2",
    )(hg, hg, hg, hg, lb_logits, gn)


def _ffn_kernel(x_ref, attn_ref, rec_ref, ga_ref, wout_ref, g2_ref, wup_ref,
                cw_ref, cb_ref, wdown_ref, gf_ref, o_ref,
                h_sc, u_sc, act_sc, carry_sc, nat_sc, *, tiles_per_seq):
    tm = x_ref.shape[0]
    i = pl.program_id(0)

    @pl.when(i % tiles_per_seq == 0)
    def _():
        carry_sc[...] = jnp.zeros_like(carry_sc)

    an = _rms(attn_ref[...], ga_ref[...])
    for r16 in range(ATTN_RESIDUES):
        for s in range(ATTN_WIDTH // LANES):
            nat_sc[s, pl.ds(r16, tm // ATTN_RESIDUES, stride=ATTN_RESIDUES), :] = (
                an[r16, :, s * LANES:(s + 1) * LANES])
    an = jnp.concatenate([nat_sc[s] for s in range(ATTN_WIDTH // LANES)], axis=1)
    mix = jnp.dot(an.astype(BF16), wout_ref[:ATTN_WIDTH, :], preferred_element_type=F32)
    mix = mix + jnp.dot(rec_ref[...], wout_ref[ATTN_WIDTH:, :],
                        preferred_element_type=F32)
    h = x_ref[...] + mix
    h_sc[...] = h
    u_sc[...] = _rms(h, g2_ref[...]).astype(BF16)

    row = lax.broadcasted_iota(jnp.int32, (tm, FF_CHUNK), 0)
    sqrt_half = math.sqrt(0.5)

    def chunk(c):
        u = u_sc[...]
        cols = slice(c * FF_CHUNK, (c + 1) * FF_CHUNK)
        vcols = slice(D_FF + c * FF_CHUNK, D_FF + (c + 1) * FF_CHUNK)
        gate = jnp.dot(u, wup_ref[:, cols], preferred_element_type=F32)
        val = jnp.dot(u, wup_ref[:, vcols], preferred_element_type=F32)
        prev = carry_sc[c]
        carry_sc[c] = gate[tm - SUBLANES:, :]
        p1 = jnp.broadcast_to(prev[SUBLANES - 1:SUBLANES], (tm, FF_CHUNK))
        p2 = jnp.broadcast_to(prev[SUBLANES - 2:SUBLANES - 1], (tm, FF_CHUNK))
        g1 = jnp.where(row == 0, p1, pltpu.roll(gate, 1, axis=0))
        g2 = jnp.where(row == 0, p2, jnp.where(row == 1, p1,
                                               pltpu.roll(gate, 2, axis=0)))
        cw = cw_ref[:, cols]
        conv = cb_ref[:, cols] + cw[0:1] * g2 + cw[1:2] * g1 + cw[2:3] * gate
        act = 0.5 * conv * (1.0 + lax.erf(conv * sqrt_half)) * val
        act_sc[:, cols] = act.astype(BF16)

    for c in range(N_FF_CHUNKS):
        chunk(c)
    down = jnp.dot(act_sc[...], wdown_ref[...], preferred_element_type=F32)
    o_ref[...] = _rms(h_sc[...] + down, gf_ref[...])


def _ffn(x2, attn, rec2, ga, wout, g2, wup, cw, cb, wdown, gf, tm, seq):
    T = x2.shape[0]
    tps = seq // tm
    const = lambda shape: pl.BlockSpec(shape, lambda i: (0,) * len(shape),
                                       pipeline_mode=pl.Buffered(1))
    return pl.pallas_call(
        functools.partial(_ffn_kernel, tiles_per_seq=tps),
        grid=(T // tm,),
        in_specs=[
            pl.BlockSpec((tm, D_MODEL), lambda i: (i, 0)),
            pl.BlockSpec((None, ATTN_RESIDUES, tm // ATTN_RESIDUES, ATTN_WIDTH),
                         lambda i: (i // tps, 0, i % tps, 0)),
            pl.BlockSpec((tm, HGRN_WIDTH), lambda i: (i, 0)),
            const((1, ATTN_WIDTH)),
            const((D_MODEL, D_MODEL)),
            const((1, D_MODEL)),
            const((D_MODEL, 2 * D_FF)),
            const((CONV_WIDTH, D_FF)),
            const((1, D_FF)),
            const((D_FF, D_MODEL)),
            const((1, D_MODEL)),
        ],
        out_specs=pl.BlockSpec((tm, D_MODEL), lambda i: (i, 0)),
        out_shape=jax.ShapeDtypeStruct((T, D_MODEL), F32),
        scratch_shapes=[
            pltpu.VMEM((tm, D_MODEL), F32),
            pltpu.VMEM((tm, D_MODEL), BF16),
            pltpu.VMEM((tm, D_FF), BF16),
            pltpu.VMEM((N_FF_CHUNKS, SUBLANES, FF_CHUNK), F32),
            pltpu.VMEM((ATTN_WIDTH // LANES, tm, LANES), F32),
        ],
        compiler_params=pltpu.CompilerParams(
            dimension_semantics=("arbitrary",), vmem_limit_bytes=VMEM_LIMIT),
        name="outproj_convglu",
    )(x2, attn, rec2, ga, wout, g2, wup, cw, cb, wdown, gf)


def kernel(x, norm1_g, w_in, attn_norm_g, hgrn_norm_g, hgrn_lb_logits, w_out,
           norm2_g, w_up, conv_w, conv_b, w_down, final_norm_g):
    B, S, D = x.shape
    depth = w_in.shape[0]
    assert depth == 1 and D == D_MODEL
    assert S % (ATTN_BLOCK * DILATIONS[-1]) == 0 and S % (HGRN_ROWS * HGRN_GROUP) == 0
    tm = 512
    assert S % tm == 0
    T = B * S
    layer = 0
    x2 = x.reshape(T, D)

    att_ops, hg = _inproj(x2, norm1_g[layer].reshape(1, D), w_in[layer].astype(BF16), tm, S)

    attn = _attention(att_ops.reshape(B, S, ATT_OPS_WIDTH), _attn_bias())
    attn = attn.reshape(B, ATTN_RESIDUES, S // ATTN_RESIDUES, ATTN_WIDTH)

    lbl = hgrn_lb_logits.astype(F32).reshape(depth + 1, HGRN_HEADS, HGRN_DIM)
    lbl = lbl.transpose(1, 0, 2)
    gn = hgrn_norm_g[layer].reshape(HGRN_HEADS, 1, HGRN_DIM)
    rec = _hgrn(hg.reshape(B, S, HG_WIDTH), lbl, gn)

    wup = w_up[layer].astype(BF16)
    cw = conv_w[layer]
    cb = conv_b[layer].reshape(1, D_FF)
    wdown = w_down[layer].astype(BF16)
    out = _ffn(x2, attn, rec.reshape(T, HGRN_WIDTH),
               attn_norm_g[layer].reshape(1, ATTN_WIDTH), w_out[layer].astype(BF16),
               norm2_g[layer].reshape(1, D), wup, cw, cb, wdown,
               final_norm_g.reshape(1, D), tm, S)
    return out.reshape(B, S, D)
```

```python
import functools
import math

import jax
import jax.numpy as jnp
from jax import lax
from jax.experimental import pallas as pl
from jax.experimental.pallas import tpu as pltpu

F32 = jnp.float32
BF16 = jnp.bfloat16

D_MODEL = 1024
ATTN_WIDTH = 512
ATTN_HEADS = 8
ATTN_HEAD_DIM = 64
ATTN_BLOCK = 128
DILATIONS = (1, 4, 16)
ATTN_RESIDUES = 16
HGRN_WIDTH = 512
HGRN_HEADS = 4
HGRN_DIM = 128
HGRN_CHUNK = 64
D_FF = 2816
CONV_WIDTH = 3
NORM_EPS = 1e-6
LOG2_E = math.log2(math.e)
QKV_WIDTH = 3 * ATTN_WIDTH
ATT_OPS_WIDTH = 5 * ATTN_WIDTH
HG_WIDTH = 4 * HGRN_WIDTH

LANES = 128
SUBLANES = 8
FF_CHUNK = 256
N_FF_CHUNKS = D_FF // FF_CHUNK
PROJ_TILE = 1024
FFN_TILE = 512
HGRN_ROWS = 256
HGRN_GROUP = 16
ATTN_GROUP = 16
VMEM_LIMIT = 56 * 1024 * 1024


def _rms(x, g):
    return x * lax.rsqrt(jnp.mean(x * x, axis=-1, keepdims=True) + NORM_EPS) * g


def _inproj_kernel(x_ref, g_ref, w_ref, att_ref, hg_ref, res_sc, tmp_sc):
    tm = x_ref.shape[0]
    run = tm // ATTN_RESIDUES
    quarter = tm // 4
    u = _rms(x_ref[...], g_ref[...]).astype(BF16)
    n_att = QKV_WIDTH // FF_CHUNK
    n_gate = HG_WIDTH // FF_CHUNK
    for c in range(max(n_att, n_gate)):
        if c < n_att:
            r = jnp.dot(u, w_ref[:, c * FF_CHUNK:(c + 1) * FF_CHUNK],
                        preferred_element_type=F32)
            res_sc[2 * c] = r[:, :LANES]
            res_sc[2 * c + 1] = r[:, LANES:]
        if c < n_gate:
            cols = slice(c * FF_CHUNK, (c + 1) * FF_CHUNK)
            wcols = slice(QKV_WIDTH + c * FF_CHUNK, QKV_WIDTH + (c + 1) * FF_CHUNK)
            hg_ref[:, cols] = jnp.dot(u, w_ref[:, wcols],
                                      preferred_element_type=F32).astype(hg_ref.dtype)
    head0 = lax.broadcasted_iota(jnp.int32, (run, LANES), 1) < ATTN_HEAD_DIM
    scale = ATTN_HEAD_DIM ** -0.5 * LOG2_E
    zero = jnp.zeros((run, LANES), F32)
    one = jnp.ones((run, LANES), F32)
    pairs = ATTN_WIDTH // LANES
    for s in range(QKV_WIDTH // LANES):
        kind, p = divmod(s, pairs)
        cols = lambda group: slice((group * pairs + p) * LANES, (group * pairs + p + 1) * LANES)
        for r4 in range(4):
            tmp_sc[s, r4 * quarter:(r4 + 1) * quarter, :] = res_sc[s, pl.ds(r4, quarter, stride=4), :]
        for r16 in range(ATTN_RESIDUES):
            rows = tmp_sc[s, pl.ds((r16 % 4) * quarter + r16 // 4, run, stride=4), :]
            if kind == 0:
                q = rows * scale
                att_ref[r16, :, cols(0)] = jnp.where(head0, q, zero).astype(BF16)
                att_ref[r16, :, cols(1)] = jnp.where(head0, zero, q).astype(BF16)
            elif kind == 1:
                att_ref[r16, :, cols(2)] = rows.astype(BF16)
            else:
                att_ref[r16, :, cols(3)] = jnp.where(head0, rows, one).astype(BF16)
                att_ref[r16, :, cols(4)] = jnp.where(head0, one, rows).astype(BF16)


def _inproj(x2, g, w, tm, seq):
    T = x2.shape[0]
    tps = seq // tm
    run = tm // ATTN_RESIDUES
    return pl.pallas_call(
        _inproj_kernel,
        grid=(T // tm,),
        in_specs=[
            pl.BlockSpec((tm, D_MODEL), lambda i: (i, 0)),
            pl.BlockSpec((1, D_MODEL), lambda i: (0, 0)),
            pl.BlockSpec((D_MODEL, QKV_WIDTH + HG_WIDTH), lambda i: (0, 0),
                         pipeline_mode=pl.Buffered(1)),
        ],
        out_specs=[
            pl.BlockSpec((None, ATTN_RESIDUES, run, ATT_OPS_WIDTH),
                         lambda i: (i // tps, 0, i % tps, 0)),
            pl.BlockSpec((tm, HG_WIDTH), lambda i: (i, 0)),
        ],
        out_shape=[
            jax.ShapeDtypeStruct((T // seq, ATTN_RESIDUES, seq // ATTN_RESIDUES,
                                  ATT_OPS_WIDTH), BF16),
            jax.ShapeDtypeStruct((T, HG_WIDTH), BF16),
        ],
        scratch_shapes=[pltpu.VMEM((QKV_WIDTH // LANES, tm, LANES), F32)] * 2,
        compiler_params=pltpu.CompilerParams(
            dimension_semantics=("arbitrary",), vmem_limit_bytes=VMEM_LIMIT),
        name="inproj",
    )(x2, g, w)


def _attn_runs(d):
    g = ATTN_RESIDUES // d
    return g, ATTN_BLOCK // g


def _attn_bias():
    blk = ATTN_BLOCK
    e = jnp.arange(2 * blk) % blk
    half = jnp.arange(2 * blk) // blk
    out = []
    for d in DILATIONS:
        g, run = _attn_runs(d)
        pos = g * (e % run) + e // run
        pos_q = pos[:, None]
        pos_k = (blk * half + pos)[None, :]
        dist = blk + pos_q - pos_k
        out.append((dist >= 0) & (dist <= blk))
        out.append(pos_k <= pos_q)
    return jnp.where(jnp.stack(out), 0.0, -jnp.inf).astype(F32)


def _attn_kernel(q0_ref, q1_ref, k_ref, v0_ref, v1_ref, bias_ref, o_ref,
                 m_sc, l_sc, acc_sc, *, seq):
    blk = ATTN_BLOCK
    n_blocks = seq // blk
    ns = seq // ATTN_RESIDUES
    head0 = lax.broadcasted_iota(jnp.int32, (blk, LANES), 1) < ATTN_HEAD_DIM

    def gather(ref, runs):
        return jnp.concatenate([ref[rs, :] for rs in runs], axis=0)

    def block_runs(d, j, which):
        nb = n_blocks // d
        g, run = _attn_runs(d)
        r, n = divmod(j, nb)
        n0 = max(n - 1, 0)
        nsel = {"q": [n], "kv": [n0, n0 + 1]}[which]
        return [pl.ds((d * jj + r) * ns + run * nn, run) for nn in nsel for jj in range(g)]

    def scores_of(group):
        out = []
        for d, j in group:
            qruns = block_runs(d, j, "q")
            qs = jnp.concatenate([gather(q0_ref, qruns),
                                  gather(q1_ref, qruns)], axis=0)
            kw = gather(k_ref, block_runs(d, j, "kv"))
            out.append(lax.dot_general(qs, kw, (((1,), (1,)), ((), ())),
                                       preferred_element_type=F32))
        return out

    def finish(group, scores):
        parts = []
        for (d, j), s in zip(group, scores):
            kruns = block_runs(d, j, "kv")
            first = 1 if j % (n_blocks // d) == 0 else 0
            s = s + bias_ref[2 * DILATIONS.index(d) + first]
            m = jnp.max(s, axis=-1, keepdims=True)
            p = jnp.exp2(s - m).astype(BF16)
            pv0 = jnp.dot(p[:blk], gather(v0_ref, kruns),
                          preferred_element_type=F32)
            pv1 = jnp.dot(p[blk:], gather(v1_ref, kruns),
                          preferred_element_type=F32)
            parts.append((pv0, pv1, m))
        for (d, j), (pv0, pv1, m) in zip(group, parts):
            bi = DILATIONS.index(d)
            qruns = block_runs(d, j, "q")
            o_b = jnp.where(head0, pv0, pv1)
            l_b = jnp.where(head0, pv1, pv0)
            m_b = jnp.where(head0, jnp.broadcast_to(m[:blk], (blk, LANES)),
                            jnp.broadcast_to(m[blk:], (blk, LANES)))
            run_len = blk // len(qruns)
            for i, rs in enumerate(qruns):
                piece = slice(i * run_len, (i + 1) * run_len)
                acc_sc[bi, rs, :] = o_b[piece]
                m_sc[bi, rs, :] = m_b[piece]
                l_sc[bi, rs, :] = l_b[piece]

    blocks = [(d, j) for d in DILATIONS for j in range(n_blocks)]
    groups = [blocks[i:i + ATTN_GROUP] for i in range(0, len(blocks), ATTN_GROUP)]
    scores = scores_of(groups[0])
    for i, group in enumerate(groups):
        nxt = scores_of(groups[i + 1]) if i + 1 < len(groups) else None
        finish(group, scores)
        scores = nxt

    def merge_body(t, carry):
        rows = pl.ds(pl.multiple_of(t * blk, blk), blk)
        ms = [m_sc[bi, rows, :] for bi in range(len(DILATIONS))]
        m_all = functools.reduce(jnp.maximum, ms)
        den = jnp.zeros((blk, LANES), F32)
        num = jnp.zeros((blk, LANES), F32)
        for bi in range(len(DILATIONS)):
            w = jnp.exp2(ms[bi] - m_all)
            den = den + w * pltpu.roll(l_sc[bi, rows, :], ATTN_HEAD_DIM, axis=1)
            num = num + w * acc_sc[bi, rows, :]
        o_ref[rows, :] = num / den
        return carry

    lax.fori_loop(0, n_blocks, merge_body, 0, unroll=4)


def _attention(att_ops, bias):
    B, S, _ = att_ops.shape
    n_pairs = ATTN_WIDTH // LANES
    blockspec = lambda group: pl.BlockSpec((None, S, LANES),
                                           lambda b, p: (b, 0, group * n_pairs + p))
    return pl.pallas_call(
        functools.partial(_attn_kernel, seq=S),
        grid=(B, n_pairs),
        in_specs=[blockspec(group) for group in range(ATT_OPS_WIDTH // ATTN_WIDTH)] + [
            pl.BlockSpec(bias.shape, lambda b, p: (0, 0, 0), pipeline_mode=pl.Buffered(1)),
        ],
        out_specs=pl.BlockSpec((None, S, LANES), lambda b, p: (b, 0, p)),
        out_shape=jax.ShapeDtypeStruct((B, S, ATTN_WIDTH), F32),
        scratch_shapes=[pltpu.VMEM((len(DILATIONS), S, LANES), F32)] * 3,
        compiler_params=pltpu.CompilerParams(
            dimension_semantics=("arbitrary", "arbitrary"),
            vmem_limit_bytes=VMEM_LIMIT),
        name="dilated_attn",
    )(*([att_ops] * (ATT_OPS_WIDTH // ATTN_WIDTH)), bias)


def _hgrn_kernel(q_ref, f_ref, i_ref, g_ref, lbl_ref, gn_ref, o_ref, st_sc, tri_sc, *, seq):
    R = HGRN_ROWS
    C = HGRN_CHUNK
    logits = lbl_ref[...]
    e = jnp.exp(logits - jnp.max(logits, axis=0, keepdims=True))
    lb = e[0:1] / jnp.sum(e, axis=0, keepdims=True)
    f_mid = 0.5 * (1.0 + lb)
    f_half = 0.5 * (1.0 - lb)
    gn = gn_ref[...]

    st_sc[...] = jnp.zeros_like(st_sc)
    row = lax.broadcasted_iota(jnp.int32, (R, R), 0)
    col = lax.broadcasted_iota(jnp.int32, (R, R), 1)
    causal = (row >= col) & ((row // C) == (col // C))
    tri_sc[...] = jnp.where(causal, 1.0, 0.0).astype(BF16)

    G = HGRN_GROUP
    NC = R // C

    def body(it, carry):
        rows = [pl.ds(pl.multiple_of((it * G + g) * R, R), R) for g in range(G)]
        ivs, keys, qfs, b2s = {}, {}, {}, {}
        q_decs, k_ends, decays, a_s = {}, {}, {}, {}
        o_intras, u_ts, o_inters = {}, {}, {}
        st = [st_sc[...]]

        def gates_and_prefix_sum(g):
            q = q_ref[rows[g], :].astype(F32)
            f = f_ref[rows[g], :].astype(F32)
            ivs[g] = i_ref[rows[g], :].astype(BF16)
            hq = 0.5 * q
            qfs[g] = hq + hq * jnp.tanh(hq)
            forget = f_mid + f_half * jnp.tanh(0.5 * f)
            keys[g] = 1.0 - forget
            lf = jnp.log2(forget)
            lf_hi = lf.astype(BF16)
            lf_lo = (lf - lf_hi.astype(F32)).astype(BF16)
            b2s[g] = jnp.dot(tri_sc[...], jnp.concatenate([lf_hi, lf_lo], axis=1),
                             preferred_element_type=F32)

        def intra_chunk_scores(g):
            b2 = b2s.pop(g)
            b = b2[:, :HGRN_DIM] + b2[:, HGRN_DIM:]
            decay_rows = [jnp.exp2(b[j * C + C - 1:j * C + C]) for j in range(NC)]
            decay_end = jnp.concatenate(
                [jnp.broadcast_to(d, (C, HGRN_DIM)) for d in decay_rows], axis=0)
            k_inv_f = keys.pop(g) * jnp.exp2(-b)
            q_decs[g] = (qfs.pop(g) * jnp.exp2(b)).astype(BF16)
            k_ends[g] = (k_inv_f * decay_end).astype(BF16)
            decays[g] = decay_rows
            a_s[g] = lax.dot_general(q_decs[g], k_inv_f.astype(BF16),
                                     (((1,), (1,)), ((), ())),
                                     preferred_element_type=F32)

        def intra_chunk_outputs(g):
            a = jnp.where(causal, a_s.pop(g), 0.0).astype(BF16)
            iv = ivs.pop(g)
            k_end = k_ends.pop(g)
            o_intras[g] = jnp.dot(a, iv, preferred_element_type=F32)
            u_ts[g] = [lax.dot_general(iv[j * C:(j + 1) * C], k_end[j * C:(j + 1) * C],
                                       (((0,), (0,)), ((), ())),
                                       preferred_element_type=F32)
                       for j in range(NC)]

        def state_scan(g):
            parts = []
            q_dec = q_decs.pop(g)
            for j in range(NC):
                parts.append(lax.dot_general(q_dec[j * C:(j + 1) * C], st[0].astype(BF16),
                                             (((1,), (1,)), ((), ())),
                                             preferred_element_type=F32))
                st[0] = st[0] * decays[g][j] + u_ts[g][j]
            o_inters[g] = jnp.concatenate(parts, axis=0)

        def emit(g):
            o = _rms(o_intras.pop(g) + o_inters.pop(g), gn)
            hg = 0.5 * g_ref[rows[g], :].astype(F32)
            o_ref[rows[g], :] = (o * (hg + hg * jnp.tanh(hg))).astype(o_ref.dtype)

        stages = (gates_and_prefix_sum, intra_chunk_scores, intra_chunk_outputs,
                  state_scan, emit)
        for t in range(G + len(stages) - 1):
            for k in reversed(range(len(stages))):
                g = t - k
                if 0 <= g < G:
                    stages[k](g)
        st_sc[...] = st[0]
        return carry

    lax.fori_loop(0, seq // (R * G), body, 0)


def _hgrn(hg, lb_logits, gn):
    B, S, _ = hg.shape
    H = HGRN_HEADS
    n_rows = lb_logits.shape[1]
    blockspec = lambda off: pl.BlockSpec((None, S, HGRN_DIM), lambda b, h: (b, 0, h + off))
    return pl.pallas_call(
        functools.partial(_hgrn_kernel, seq=S),
        grid=(B, H),
        in_specs=[
            blockspec(0), blockspec(H), blockspec(2 * H), blockspec(3 * H),
            pl.BlockSpec((None, n_rows, HGRN_DIM), lambda b, h: (h, 0, 0)),
            pl.BlockSpec((None, 1, HGRN_DIM), lambda b, h: (h, 0, 0)),
        ],
        out_specs=pl.BlockSpec((None, S, HGRN_DIM), lambda b, h: (b, 0, h)),
        out_shape=jax.ShapeDtypeStruct((B, S, HGRN_WIDTH), BF16),
        scratch_shapes=[pltpu.VMEM((HGRN_DIM, HGRN_DIM), F32),
                        pltpu.VMEM((HGRN_ROWS, HGRN_ROWS), BF16)],
        compiler_params=pltpu.CompilerParams(
            dimension_semantics=("arbitrary", "arbitrary"),
            vmem_limit_bytes=VMEM_LIMIT),
        name="hgrn2",
    )(hg, hg, hg, hg, lb_logits, gn)


def _ffn_kernel(x_ref, attn_ref, rec_ref, ga_ref, wout_ref, g2_ref, wup_ref,
                cw_ref, cb_ref, wdown_ref, gf_ref, o_ref,
                h_sc, u_sc, act_sc, carry_sc, nat_sc, *, tiles_per_seq):
    tm = x_ref.shape[0]
    i = pl.program_id(0)

    @pl.when(i % tiles_per_seq == 0)
    def _():
        carry_sc[...] = jnp.zeros_like(carry_sc)

    an = _rms(attn_ref[...], ga_ref[...])
    for r16 in range(ATTN_RESIDUES):
        for s in range(ATTN_WIDTH // LANES):
            nat_sc[s, pl.ds(r16, tm // ATTN_RESIDUES, stride=ATTN_RESIDUES), :] = (
                an[r16, :, s * LANES:(s + 1) * LANES])
    an = jnp.concatenate([nat_sc[s] for s in range(ATTN_WIDTH // LANES)], axis=1)
    mix = jnp.dot(an.astype(BF16), wout_ref[:ATTN_WIDTH, :], preferred_element_type=F32)
    mix = mix + jnp.dot(rec_ref[...], wout_ref[ATTN_WIDTH:, :],
                        preferred_element_type=F32)
    h = x_ref[...] + mix
    h_sc[...] = h
    u_sc[...] = _rms(h, g2_ref[...]).astype(BF16)

    row = lax.broadcasted_iota(jnp.int32, (tm, FF_CHUNK), 0)
    sqrt_half = math.sqrt(0.5)

    def chunk(c):
        u = u_sc[...]
        cols = slice(c * FF_CHUNK, (c + 1) * FF_CHUNK)
        vcols = slice(D_FF + c * FF_CHUNK, D_FF + (c + 1) * FF_CHUNK)
        gate = jnp.dot(u, wup_ref[:, cols], preferred_element_type=F32)
        val = jnp.dot(u, wup_ref[:, vcols], preferred_element_type=F32)
        prev = carry_sc[c]
        carry_sc[c] = gate[tm - SUBLANES:, :]
        p1 = jnp.broadcast_to(prev[SUBLANES - 1:SUBLANES], (tm, FF_CHUNK))
        p2 = jnp.broadcast_to(prev[SUBLANES - 2:SUBLANES - 1], (tm, FF_CHUNK))
        g1 = jnp.where(row == 0, p1, pltpu.roll(gate, 1, axis=0))
        g2 = jnp.where(row == 0, p2, jnp.where(row == 1, p1,
                                               pltpu.roll(gate, 2, axis=0)))
        cw = cw_ref[:, cols]
        conv = cb_ref[:, cols] + cw[0:1] * g2 + cw[1:2] * g1 + cw[2:3] * gate
        act = 0.5 * conv * (1.0 + lax.erf(conv * sqrt_half)) * val
        act_sc[:, cols] = act.astype(BF16)

    for c in range(N_FF_CHUNKS):
        chunk(c)
    down = jnp.dot(act_sc[...], wdown_ref[...], preferred_element_type=F32)
    o_ref[...] = _rms(h_sc[...] + down, gf_ref[...])


def _ffn(x2, attn, rec2, ga, wout, g2, wup, cw, cb, wdown, gf, tm, seq):
    T = x2.shape[0]
    tps = seq // tm
    const = lambda shape: pl.BlockSpec(shape, lambda i: (0,) * len(shape),
                                       pipeline_mode=pl.Buffered(1))
    return pl.pallas_call(
        functools.partial(_ffn_kernel, tiles_per_seq=tps),
        grid=(T // tm,),
        in_specs=[
            pl.BlockSpec((tm, D_MODEL), lambda i: (i, 0)),
            pl.BlockSpec((None, ATTN_RESIDUES, tm // ATTN_RESIDUES, ATTN_WIDTH),
                         lambda i: (i // tps, 0, i % tps, 0)),
            pl.BlockSpec((tm, HGRN_WIDTH), lambda i: (i, 0)),
            const((1, ATTN_WIDTH)),
            const((D_MODEL, D_MODEL)),
            const((1, D_MODEL)),
            const((D_MODEL, 2 * D_FF)),
            const((CONV_WIDTH, D_FF)),
            const((1, D_FF)),
            const((D_FF, D_MODEL)),
            const((1, D_MODEL)),
        ],
        out_specs=pl.BlockSpec((tm, D_MODEL), lambda i: (i, 0)),
        out_shape=jax.ShapeDtypeStruct((T, D_MODEL), F32),
        scratch_shapes=[
            pltpu.VMEM((tm, D_MODEL), F32),
            pltpu.VMEM((tm, D_MODEL), BF16),
            pltpu.VMEM((tm, D_FF), BF16),
            pltpu.VMEM((N_FF_CHUNKS, SUBLANES, FF_CHUNK), F32),
            pltpu.VMEM((ATTN_WIDTH // LANES, tm, LANES), F32),
        ],
        compiler_params=pltpu.CompilerParams(
            dimension_semantics=("arbitrary",), vmem_limit_bytes=VMEM_LIMIT),
        name="outproj_convglu",
    )(x2, attn, rec2, ga, wout, g2, wup, cw, cb, wdown, gf)


def kernel(x, norm1_g, w_in, attn_norm_g, hgrn_norm_g, hgrn_lb_logits, w_out,
           norm2_g, w_up, conv_w, conv_b, w_down, final_norm_g):
    B, S, D = x.shape
    depth = w_in.shape[0]
    assert depth == 1 and D == D_MODEL
    assert S % (ATTN_BLOCK * DILATIONS[-1]) == 0 and S % (HGRN_ROWS * HGRN_GROUP) == 0
    assert S % PROJ_TILE == 0 and S % FFN_TILE == 0
    T = B * S
    layer = 0
    x2 = x.reshape(T, D)

    att_ops, hg = _inproj(x2, norm1_g[layer].reshape(1, D), w_in[layer].astype(BF16),
                          PROJ_TILE, S)

    attn = _attention(att_ops.reshape(B, S, ATT_OPS_WIDTH), _attn_bias())
    attn = attn.reshape(B, ATTN_RESIDUES, S // ATTN_RESIDUES, ATTN_WIDTH)

    lbl = hgrn_lb_logits.astype(F32).reshape(depth + 1, HGRN_HEADS, HGRN_DIM)
    lbl = lbl.transpose(1, 0, 2)
    gn = hgrn_norm_g[layer].reshape(HGRN_HEADS, 1, HGRN_DIM)
    rec = _hgrn(hg.reshape(B, S, HG_WIDTH), lbl, gn)

    wup = w_up[layer].astype(BF16)
    cw = conv_w[layer]
    cb = conv_b[layer].reshape(1, D_FF)
    wdown = w_down[layer].astype(BF16)
    out = _ffn(x2, attn, rec.reshape(T, HGRN_WIDTH),
               attn_norm_g[layer].reshape(1, ATTN_WIDTH), w_out[layer].astype(BF16),
               norm2_g[layer].reshape(1, D), wup, cw, cb, wdown,
               final_norm_g.reshape(1, D), FFN_TILE, S)
    return out.reshape(B, S, D)
```

```python
import functools
import math

import jax
import jax.numpy as jnp
from jax import lax
from jax.experimental import pallas as pl
from jax.experimental.pallas import tpu as pltpu

F32 = jnp.float32
BF16 = jnp.bfloat16

D_MODEL = 1024
ATTN_WIDTH = 512
ATTN_HEADS = 8
ATTN_HEAD_DIM = 64
ATTN_BLOCK = 128
DILATIONS = (1, 4, 16)
ATTN_RESIDUES = 16
HGRN_WIDTH = 512
HGRN_HEADS = 4
HGRN_DIM = 128
HGRN_CHUNK = 64
D_FF = 2816
CONV_WIDTH = 3
NORM_EPS = 1e-6
LOG2_E = math.log2(math.e)
QKV_WIDTH = 3 * ATTN_WIDTH
ATT_OPS_WIDTH = 5 * ATTN_WIDTH
HG_WIDTH = 4 * HGRN_WIDTH

LANES = 128
SUBLANES = 8
FF_CHUNK = 256
FF_CHUNKS = (768, 768, 768, 512)
PROJ_TILE = 1024
FFN_TILE = 512
HGRN_ROWS = 256
HGRN_GROUP = 16
ATTN_GROUP = 16
VMEM_LIMIT = 56 * 1024 * 1024


def _rms(x, g):
    return x * lax.rsqrt(jnp.mean(x * x, axis=-1, keepdims=True) + NORM_EPS) * g


def _inproj_kernel(x_ref, g_ref, w_ref, att_ref, hg_ref, res_sc, tmp_sc):
    tm = x_ref.shape[0]
    run = tm // ATTN_RESIDUES
    quarter = tm // 4
    u = _rms(x_ref[...], g_ref[...]).astype(BF16)
    n_att = QKV_WIDTH // FF_CHUNK
    n_gate = HG_WIDTH // FF_CHUNK
    for c in range(max(n_att, n_gate)):
        if c < n_att:
            r = jnp.dot(u, w_ref[:, c * FF_CHUNK:(c + 1) * FF_CHUNK],
                        preferred_element_type=F32)
            res_sc[2 * c] = r[:, :LANES]
            res_sc[2 * c + 1] = r[:, LANES:]
        if c < n_gate:
            cols = slice(c * FF_CHUNK, (c + 1) * FF_CHUNK)
            wcols = slice(QKV_WIDTH + c * FF_CHUNK, QKV_WIDTH + (c + 1) * FF_CHUNK)
            hg_ref[:, cols] = jnp.dot(u, w_ref[:, wcols],
                                      preferred_element_type=F32).astype(hg_ref.dtype)
    head0 = lax.broadcasted_iota(jnp.int32, (run, LANES), 1) < ATTN_HEAD_DIM
    scale = ATTN_HEAD_DIM ** -0.5 * LOG2_E
    zero = jnp.zeros((run, LANES), F32)
    one = jnp.ones((run, LANES), F32)
    pairs = ATTN_WIDTH // LANES
    for s in range(QKV_WIDTH // LANES):
        kind, p = divmod(s, pairs)
        cols = lambda group: slice((group * pairs + p) * LANES, (group * pairs + p + 1) * LANES)
        for r4 in range(4):
            tmp_sc[s, r4 * quarter:(r4 + 1) * quarter, :] = res_sc[s, pl.ds(r4, quarter, stride=4), :]
        for r16 in range(ATTN_RESIDUES):
            rows = tmp_sc[s, pl.ds((r16 % 4) * quarter + r16 // 4, run, stride=4), :]
            if kind == 0:
                q = rows * scale
                att_ref[r16, :, cols(0)] = jnp.where(head0, q, zero).astype(BF16)
                att_ref[r16, :, cols(1)] = jnp.where(head0, zero, q).astype(BF16)
            elif kind == 1:
                att_ref[r16, :, cols(2)] = rows.astype(BF16)
            else:
                att_ref[r16, :, cols(3)] = jnp.where(head0, rows, one).astype(BF16)
                att_ref[r16, :, cols(4)] = jnp.where(head0, one, rows).astype(BF16)


def _inproj(x2, g, w, tm, seq):
    T = x2.shape[0]
    tps = seq // tm
    run = tm // ATTN_RESIDUES
    return pl.pallas_call(
        _inproj_kernel,
        grid=(T // tm,),
        in_specs=[
            pl.BlockSpec((tm, D_MODEL), lambda i: (i, 0)),
            pl.BlockSpec((1, D_MODEL), lambda i: (0, 0)),
            pl.BlockSpec((D_MODEL, QKV_WIDTH + HG_WIDTH), lambda i: (0, 0),
                         pipeline_mode=pl.Buffered(1)),
        ],
        out_specs=[
            pl.BlockSpec((None, ATTN_RESIDUES, run, ATT_OPS_WIDTH),
                         lambda i: (i // tps, 0, i % tps, 0)),
            pl.BlockSpec((tm, HG_WIDTH), lambda i: (i, 0)),
        ],
        out_shape=[
            jax.ShapeDtypeStruct((T // seq, ATTN_RESIDUES, seq // ATTN_RESIDUES,
                                  ATT_OPS_WIDTH), BF16),
            jax.ShapeDtypeStruct((T, HG_WIDTH), BF16),
        ],
        scratch_shapes=[pltpu.VMEM((QKV_WIDTH // LANES, tm, LANES), F32)] * 2,
        compiler_params=pltpu.CompilerParams(
            dimension_semantics=("arbitrary",), vmem_limit_bytes=VMEM_LIMIT),
        name="inproj",
    )(x2, g, w)


def _attn_runs(d):
    g = ATTN_RESIDUES // d
    return g, ATTN_BLOCK // g


def _attn_bias():
    blk = ATTN_BLOCK
    e = jnp.arange(2 * blk) % blk
    half = jnp.arange(2 * blk) // blk
    out = []
    for d in DILATIONS:
        g, run = _attn_runs(d)
        pos = g * (e % run) + e // run
        pos_q = pos[:, None]
        pos_k = (blk * half + pos)[None, :]
        dist = blk + pos_q - pos_k
        out.append((dist >= 0) & (dist <= blk))
        out.append(pos_k <= pos_q)
    return jnp.where(jnp.stack(out), 0.0, -jnp.inf).astype(F32)


def _attn_kernel(q0_ref, q1_ref, k_ref, v0_ref, v1_ref, bias_ref, o_ref,
                 m_sc, l_sc, acc_sc, *, seq):
    blk = ATTN_BLOCK
    n_blocks = seq // blk
    ns = seq // ATTN_RESIDUES
    head0 = lax.broadcasted_iota(jnp.int32, (blk, LANES), 1) < ATTN_HEAD_DIM

    def gather(ref, runs):
        return jnp.concatenate([ref[rs, :] for rs in runs], axis=0)

    def block_runs(d, j, which):
        nb = n_blocks // d
        g, run = _attn_runs(d)
        r, n = divmod(j, nb)
        n0 = max(n - 1, 0)
        nsel = {"q": [n], "kv": [n0, n0 + 1]}[which]
        return [pl.ds((d * jj + r) * ns + run * nn, run) for nn in nsel for jj in range(g)]

    def scores_of(group):
        out = []
        for d, j in group:
            qruns = block_runs(d, j, "q")
            qs = jnp.concatenate([gather(q0_ref, qruns),
                                  gather(q1_ref, qruns)], axis=0)
            kw = gather(k_ref, block_runs(d, j, "kv"))
            out.append(lax.dot_general(qs, kw, (((1,), (1,)), ((), ())),
                                       preferred_element_type=F32))
        return out

    def finish(group, scores):
        parts = []
        for (d, j), s in zip(group, scores):
            kruns = block_runs(d, j, "kv")
            first = 1 if j % (n_blocks // d) == 0 else 0
            s = s + bias_ref[2 * DILATIONS.index(d) + first]
            m = jnp.max(s, axis=-1, keepdims=True)
            p = jnp.exp2(s - m).astype(BF16)
            pv0 = jnp.dot(p[:blk], gather(v0_ref, kruns),
                          preferred_element_type=F32)
            pv1 = jnp.dot(p[blk:], gather(v1_ref, kruns),
                          preferred_element_type=F32)
            parts.append((pv0, pv1, m))
        for (d, j), (pv0, pv1, m) in zip(group, parts):
            bi = DILATIONS.index(d)
            qruns = block_runs(d, j, "q")
            o_b = jnp.where(head0, pv0, pv1)
            l_b = jnp.where(head0, pv1, pv0)
            m_b = jnp.where(head0, jnp.broadcast_to(m[:blk], (blk, LANES)),
                            jnp.broadcast_to(m[blk:], (blk, LANES)))
            run_len = blk // len(qruns)
            for i, rs in enumerate(qruns):
                piece = slice(i * run_len, (i + 1) * run_len)
                acc_sc[bi, rs, :] = o_b[piece]
                m_sc[bi, rs, :] = m_b[piece]
                l_sc[bi, rs, :] = l_b[piece]

    blocks = [(d, j) for d in DILATIONS for j in range(n_blocks)]
    groups = [blocks[i:i + ATTN_GROUP] for i in range(0, len(blocks), ATTN_GROUP)]
    scores = scores_of(groups[0])
    for i, group in enumerate(groups):
        nxt = scores_of(groups[i + 1]) if i + 1 < len(groups) else None
        finish(group, scores)
        scores = nxt

    def merge_body(t, carry):
        rows = pl.ds(pl.multiple_of(t * blk, blk), blk)
        ms = [m_sc[bi, rows, :] for bi in range(len(DILATIONS))]
        m_all = functools.reduce(jnp.maximum, ms)
        den = jnp.zeros((blk, LANES), F32)
        num = jnp.zeros((blk, LANES), F32)
        for bi in range(len(DILATIONS)):
            w = jnp.exp2(ms[bi] - m_all)
            den = den + w * pltpu.roll(l_sc[bi, rows, :], ATTN_HEAD_DIM, axis=1)
            num = num + w * acc_sc[bi, rows, :]
        o_ref[rows, :] = num / den
        return carry

    lax.fori_loop(0, n_blocks, merge_body, 0, unroll=4)


def _attention(att_ops, bias):
    B, S, _ = att_ops.shape
    n_pairs = ATTN_WIDTH // LANES
    blockspec = lambda group: pl.BlockSpec((None, S, LANES),
                                           lambda b, p: (b, 0, group * n_pairs + p))
    return pl.pallas_call(
        functools.partial(_attn_kernel, seq=S),
        grid=(B, n_pairs),
        in_specs=[blockspec(group) for group in range(ATT_OPS_WIDTH // ATTN_WIDTH)] + [
            pl.BlockSpec(bias.shape, lambda b, p: (0, 0, 0), pipeline_mode=pl.Buffered(1)),
        ],
        out_specs=pl.BlockSpec((None, S, LANES), lambda b, p: (b, 0, p)),
        out_shape=jax.ShapeDtypeStruct((B, S, ATTN_WIDTH), F32),
        scratch_shapes=[pltpu.VMEM((len(DILATIONS), S, LANES), F32)] * 3,
        compiler_params=pltpu.CompilerParams(
            dimension_semantics=("arbitrary", "arbitrary"),
            vmem_limit_bytes=VMEM_LIMIT),
        name="dilated_attn",
    )(*([att_ops] * (ATT_OPS_WIDTH // ATTN_WIDTH)), bias)


def _hgrn_kernel(q_ref, f_ref, i_ref, g_ref, lbl_ref, gn_ref, o_ref, st_sc, tri_sc, *, seq):
    R = HGRN_ROWS
    C = HGRN_CHUNK
    logits = lbl_ref[...]
    e = jnp.exp(logits - jnp.max(logits, axis=0, keepdims=True))
    lb = e[0:1] / jnp.sum(e, axis=0, keepdims=True)
    f_mid = 0.5 * (1.0 + lb)
    f_half = 0.5 * (1.0 - lb)
    gn = gn_ref[...]

    st_sc[...] = jnp.zeros_like(st_sc)
    row = lax.broadcasted_iota(jnp.int32, (R, R), 0)
    col = lax.broadcasted_iota(jnp.int32, (R, R), 1)
    causal = (row >= col) & ((row // C) == (col // C))
    tri_sc[...] = jnp.where(causal, 1.0, 0.0).astype(BF16)

    G = HGRN_GROUP
    NC = R // C

    def body(it, carry):
        rows = [pl.ds(pl.multiple_of((it * G + g) * R, R), R) for g in range(G)]
        ivs, keys, qfs, b2s = {}, {}, {}, {}
        q_decs, k_ends, decays, a_s = {}, {}, {}, {}
        o_intras, u_ts, o_inters = {}, {}, {}
        st = [st_sc[...]]

        def gates_and_prefix_sum(g):
            q = q_ref[rows[g], :].astype(F32)
            f = f_ref[rows[g], :].astype(F32)
            ivs[g] = i_ref[rows[g], :].astype(BF16)
            hq = 0.5 * q
            qfs[g] = hq + hq * jnp.tanh(hq)
            forget = f_mid + f_half * jnp.tanh(0.5 * f)
            keys[g] = 1.0 - forget
            lf = jnp.log2(forget)
            lf_hi = lf.astype(BF16)
            lf_lo = (lf - lf_hi.astype(F32)).astype(BF16)
            b2s[g] = jnp.dot(tri_sc[...], jnp.concatenate([lf_hi, lf_lo], axis=1),
                             preferred_element_type=F32)

        def intra_chunk_scores(g):
            b2 = b2s.pop(g)
            b = b2[:, :HGRN_DIM] + b2[:, HGRN_DIM:]
            decay_rows = [jnp.exp2(b[j * C + C - 1:j * C + C]) for j in range(NC)]
            decay_end = jnp.concatenate(
                [jnp.broadcast_to(d, (C, HGRN_DIM)) for d in decay_rows], axis=0)
            k_inv_f = keys.pop(g) * jnp.exp2(-b)
            q_decs[g] = (qfs.pop(g) * jnp.exp2(b)).astype(BF16)
            k_ends[g] = (k_inv_f * decay_end).astype(BF16)
            decays[g] = decay_rows
            a_s[g] = lax.dot_general(q_decs[g], k_inv_f.astype(BF16),
                                     (((1,), (1,)), ((), ())),
                                     preferred_element_type=F32)

        def intra_chunk_outputs(g):
            a = jnp.where(causal, a_s.pop(g), 0.0).astype(BF16)
            iv = ivs.pop(g)
            k_end = k_ends.pop(g)
            o_intras[g] = jnp.dot(a, iv, preferred_element_type=F32)
            u_ts[g] = [lax.dot_general(iv[j * C:(j + 1) * C], k_end[j * C:(j + 1) * C],
                                       (((0,), (0,)), ((), ())),
                                       preferred_element_type=F32)
                       for j in range(NC)]

        def state_scan(g):
            parts = []
            q_dec = q_decs.pop(g)
            for j in range(NC):
                parts.append(lax.dot_general(q_dec[j * C:(j + 1) * C], st[0].astype(BF16),
                                             (((1,), (1,)), ((), ())),
                                             preferred_element_type=F32))
                st[0] = st[0] * decays[g][j] + u_ts[g][j]
            o_inters[g] = jnp.concatenate(parts, axis=0)

        def emit(g):
            o = _rms(o_intras.pop(g) + o_inters.pop(g), gn)
            hg = 0.5 * g_ref[rows[g], :].astype(F32)
            o_ref[rows[g], :] = (o * (hg + hg * jnp.tanh(hg))).astype(o_ref.dtype)

        stages = (gates_and_prefix_sum, intra_chunk_scores, intra_chunk_outputs,
                  state_scan, emit)
        for t in range(G + len(stages) - 1):
            for k in reversed(range(len(stages))):
                g = t - k
                if 0 <= g < G:
                    stages[k](g)
        st_sc[...] = st[0]
        return carry

    lax.fori_loop(0, seq // (R * G), body, 0)


def _hgrn(hg, lb_logits, gn):
    B, S, _ = hg.shape
    H = HGRN_HEADS
    n_rows = lb_logits.shape[1]
    blockspec = lambda off: pl.BlockSpec((None, S, HGRN_DIM), lambda b, h: (b, 0, h + off))
    return pl.pallas_call(
        functools.partial(_hgrn_kernel, seq=S),
        grid=(B, H),
        in_specs=[
            blockspec(0), blockspec(H), blockspec(2 * H), blockspec(3 * H),
            pl.BlockSpec((None, n_rows, HGRN_DIM), lambda b, h: (h, 0, 0)),
            pl.BlockSpec((None, 1, HGRN_DIM), lambda b, h: (h, 0, 0)),
        ],
        out_specs=pl.BlockSpec((None, S, HGRN_DIM), lambda b, h: (b, 0, h)),
        out_shape=jax.ShapeDtypeStruct((B, S, HGRN_WIDTH), BF16),
        scratch_shapes=[pltpu.VMEM((HGRN_DIM, HGRN_DIM), F32),
                        pltpu.VMEM((HGRN_ROWS, HGRN_ROWS), BF16)],
        compiler_params=pltpu.CompilerParams(
            dimension_semantics=("arbitrary", "arbitrary"),
            vmem_limit_bytes=VMEM_LIMIT),
        name="hgrn2",
    )(hg, hg, hg, hg, lb_logits, gn)


def _ffn_kernel(x_ref, attn_ref, rec_ref, ga_ref, wout_ref, g2_ref, wup_ref,
                cw_ref, cb_ref, wdown_ref, gf_ref, o_ref,
                h_sc, u_sc, act_sc, carry_sc, nat_sc, *, tiles_per_seq):
    tm = x_ref.shape[0]
    i = pl.program_id(0)

    @pl.when(i % tiles_per_seq == 0)
    def _():
        carry_sc[...] = jnp.zeros_like(carry_sc)

    an = _rms(attn_ref[...], ga_ref[...])
    for r16 in range(ATTN_RESIDUES):
        for s in range(ATTN_WIDTH // LANES):
            nat_sc[s, pl.ds(r16, tm // ATTN_RESIDUES, stride=ATTN_RESIDUES), :] = (
                an[r16, :, s * LANES:(s + 1) * LANES])
    an = jnp.concatenate([nat_sc[s] for s in range(ATTN_WIDTH // LANES)], axis=1)
    mix = jnp.dot(an.astype(BF16), wout_ref[:ATTN_WIDTH, :], preferred_element_type=F32)
    mix = mix + jnp.dot(rec_ref[...], wout_ref[ATTN_WIDTH:, :],
                        preferred_element_type=F32)
    h = x_ref[...] + mix
    h_sc[...] = h
    u_sc[...] = _rms(h, g2_ref[...]).astype(BF16)

    sqrt_half = math.sqrt(0.5)

    def chunk(start, width):
        u = u_sc[...]
        row = lax.broadcasted_iota(jnp.int32, (tm, width), 0)
        cols = slice(start, start + width)
        vcols = slice(D_FF + start, D_FF + start + width)
        gate = jnp.dot(u, wup_ref[:, cols], preferred_element_type=F32)
        val = jnp.dot(u, wup_ref[:, vcols], preferred_element_type=F32)
        prev = carry_sc[:, cols]
        carry_sc[:, cols] = gate[tm - SUBLANES:, :]
        p1 = jnp.broadcast_to(prev[SUBLANES - 1:SUBLANES], (tm, width))
        p2 = jnp.broadcast_to(prev[SUBLANES - 2:SUBLANES - 1], (tm, width))
        g1 = jnp.where(row == 0, p1, pltpu.roll(gate, 1, axis=0))
        g2 = jnp.where(row == 0, p2, jnp.where(row == 1, p1,
                                               pltpu.roll(gate, 2, axis=0)))
        cw = cw_ref[:, cols]
        conv = cb_ref[:, cols] + cw[0:1] * g2 + cw[1:2] * g1 + cw[2:3] * gate
        act = 0.5 * conv * (1.0 + lax.erf(conv * sqrt_half)) * val
        act_sc[:, cols] = act.astype(BF16)

    start = 0
    for width in FF_CHUNKS:
        chunk(start, width)
        start += width
    down = jnp.dot(act_sc[...], wdown_ref[...], preferred_element_type=F32)
    o_ref[...] = _rms(h_sc[...] + down, gf_ref[...])


def _ffn(x2, attn, rec2, ga, wout, g2, wup, cw, cb, wdown, gf, tm, seq):
    T = x2.shape[0]
    tps = seq // tm
    const = lambda shape: pl.BlockSpec(shape, lambda i: (0,) * len(shape),
                                       pipeline_mode=pl.Buffered(1))
    return pl.pallas_call(
        functools.partial(_ffn_kernel, tiles_per_seq=tps),
        grid=(T // tm,),
        in_specs=[
            pl.BlockSpec((tm, D_MODEL), lambda i: (i, 0)),
            pl.BlockSpec((None, ATTN_RESIDUES, tm // ATTN_RESIDUES, ATTN_WIDTH),
                         lambda i: (i // tps, 0, i % tps, 0)),
            pl.BlockSpec((tm, HGRN_WIDTH), lambda i: (i, 0)),
            const((1, ATTN_WIDTH)),
            const((D_MODEL, D_MODEL)),
            const((1, D_MODEL)),
            const((D_MODEL, 2 * D_FF)),
            const((CONV_WIDTH, D_FF)),
            const((1, D_FF)),
            const((D_FF, D_MODEL)),
            const((1, D_MODEL)),
        ],
        out_specs=pl.BlockSpec((tm, D_MODEL), lambda i: (i, 0)),
        out_shape=jax.ShapeDtypeStruct((T, D_MODEL), F32),
        scratch_shapes=[
            pltpu.VMEM((tm, D_MODEL), F32),
            pltpu.VMEM((tm, D_MODEL), BF16),
            pltpu.VMEM((tm, D_FF), BF16),
            pltpu.VMEM((SUBLANES, D_FF), F32),
            pltpu.VMEM((ATTN_WIDTH // LANES, tm, LANES), F32),
        ],
        compiler_params=pltpu.CompilerParams(
            dimension_semantics=("arbitrary",), vmem_limit_bytes=VMEM_LIMIT),
        name="outproj_convglu",
    )(x2, attn, rec2, ga, wout, g2, wup, cw, cb, wdown, gf)


def kernel(x, norm1_g, w_in, attn_norm_g, hgrn_norm_g, hgrn_lb_logits, w_out,
           norm2_g, w_up, conv_w, conv_b, w_down, final_norm_g):
    B, S, D = x.shape
    depth = w_in.shape[0]
    assert depth == 1 and D == D_MODEL
    assert S % (ATTN_BLOCK * DILATIONS[-1]) == 0 and S % (HGRN_ROWS * HGRN_GROUP) == 0
    assert S % PROJ_TILE == 0 and S % FFN_TILE == 0
    T = B * S
    layer = 0
    x2 = x.reshape(T, D)

    att_ops, hg = _inproj(x2, norm1_g[layer].reshape(1, D), w_in[layer].astype(BF16),
                          PROJ_TILE, S)

    attn = _attention(att_ops.reshape(B, S, ATT_OPS_WIDTH), _attn_bias())
    attn = attn.reshape(B, ATTN_RESIDUES, S // ATTN_RESIDUES, ATTN_WIDTH)

    lbl = hgrn_lb_logits.astype(F32).reshape(depth + 1, HGRN_HEADS, HGRN_DIM)
    lbl = lbl.transpose(1, 0, 2)
    gn = hgrn_norm_g[layer].reshape(HGRN_HEADS, 1, HGRN_DIM)
    rec = _hgrn(hg.reshape(B, S, HG_WIDTH), lbl, gn)

    wup = w_up[layer].astype(BF16)
    cw = conv_w[layer]
    cb = conv_b[layer].reshape(1, D_FF)
    wdown = w_down[layer].astype(BF16)
    out = _ffn(x2, attn, rec.reshape(T, HGRN_WIDTH),
               attn_norm_g[layer].reshape(1, ATTN_WIDTH), w_out[layer].astype(BF16),
               norm2_g[layer].reshape(1, D), wup, cw, cb, wdown,
               final_norm_g.reshape(1, D), FFN_TILE, S)
    return out.reshape(B, S, D)
```

```python
import functools
import math

import jax
import jax.numpy as jnp
from jax import lax
from jax.experimental import pallas as pl
from jax.experimental.pallas import tpu as pltpu

F32 = jnp.float32
BF16 = jnp.bfloat16

D_MODEL = 1024
ATTN_WIDTH = 512
ATTN_HEADS = 8
ATTN_HEAD_DIM = 64
ATTN_BLOCK = 128
DILATIONS = (1, 4, 16)
ATTN_RESIDUES = 16
HGRN_WIDTH = 512
HGRN_HEADS = 4
HGRN_DIM = 128
HGRN_CHUNK = 64
D_FF = 2816
CONV_WIDTH = 3
NORM_EPS = 1e-6
LOG2_E = math.log2(math.e)
QKV_WIDTH = 3 * ATTN_WIDTH
ATT_OPS_WIDTH = 5 * ATTN_WIDTH
HG_WIDTH = 4 * HGRN_WIDTH

LANES = 128
SUBLANES = 8
FF_CHUNK = 256
FF_CHUNKS = (768, 768, 768, 512)
PROJ_TILE = 1024
FFN_TILE = 512
HGRN_ROWS = 256
HGRN_GROUP = 16
ATTN_GROUP = 16
VMEM_LIMIT = 56 * 1024 * 1024


def _rms(x, g):
    return x * lax.rsqrt(jnp.mean(x * x, axis=-1, keepdims=True) + NORM_EPS) * g


def _inproj_kernel(x_ref, g_ref, w_ref, att_ref, hg_ref, res_sc, tmp_sc):
    tm = x_ref.shape[0]
    run = tm // ATTN_RESIDUES
    quarter = tm // 4
    u = _rms(x_ref[...], g_ref[...]).astype(BF16)
    n_att = QKV_WIDTH // FF_CHUNK
    n_gate = HG_WIDTH // FF_CHUNK
    for c in range(max(n_att, n_gate)):
        if c < n_att:
            r = jnp.dot(u, w_ref[:, c * FF_CHUNK:(c + 1) * FF_CHUNK],
                        preferred_element_type=F32)
            res_sc[2 * c] = r[:, :LANES]
            res_sc[2 * c + 1] = r[:, LANES:]
        if c < n_gate:
            cols = slice(c * FF_CHUNK, (c + 1) * FF_CHUNK)
            wcols = slice(QKV_WIDTH + c * FF_CHUNK, QKV_WIDTH + (c + 1) * FF_CHUNK)
            hg_ref[:, cols] = jnp.dot(u, w_ref[:, wcols],
                                      preferred_element_type=F32).astype(hg_ref.dtype)
    head0 = lax.broadcasted_iota(jnp.int32, (run, LANES), 1) < ATTN_HEAD_DIM
    scale = ATTN_HEAD_DIM ** -0.5 * LOG2_E
    zero = jnp.zeros((run, LANES), F32)
    one = jnp.ones((run, LANES), F32)
    pairs = ATTN_WIDTH // LANES
    for s in range(QKV_WIDTH // LANES):
        kind, p = divmod(s, pairs)
        cols = lambda group: slice((group * pairs + p) * LANES, (group * pairs + p + 1) * LANES)
        for r4 in range(4):
            tmp_sc[s, r4 * quarter:(r4 + 1) * quarter, :] = res_sc[s, pl.ds(r4, quarter, stride=4), :]
        for r16 in range(ATTN_RESIDUES):
            rows = tmp_sc[s, pl.ds((r16 % 4) * quarter + r16 // 4, run, stride=4), :]
            if kind == 0:
                q = rows * scale
                att_ref[r16, :, cols(0)] = jnp.where(head0, q, zero).astype(BF16)
                att_ref[r16, :, cols(1)] = jnp.where(head0, zero, q).astype(BF16)
            elif kind == 1:
                att_ref[r16, :, cols(2)] = rows.astype(BF16)
            else:
                att_ref[r16, :, cols(3)] = jnp.where(head0, rows, one).astype(BF16)
                att_ref[r16, :, cols(4)] = jnp.where(head0, one, rows).astype(BF16)


def _inproj(x2, g, w, tm, seq):
    T = x2.shape[0]
    tps = seq // tm
    run = tm // ATTN_RESIDUES
    return pl.pallas_call(
        _inproj_kernel,
        grid=(T // tm,),
        in_specs=[
            pl.BlockSpec((tm, D_MODEL), lambda i: (i, 0)),
            pl.BlockSpec((1, D_MODEL), lambda i: (0, 0)),
            pl.BlockSpec((D_MODEL, QKV_WIDTH + HG_WIDTH), lambda i: (0, 0),
                         pipeline_mode=pl.Buffered(1)),
        ],
        out_specs=[
            pl.BlockSpec((None, ATTN_RESIDUES, run, ATT_OPS_WIDTH),
                         lambda i: (i // tps, 0, i % tps, 0)),
            pl.BlockSpec((tm, HG_WIDTH), lambda i: (i, 0)),
        ],
        out_shape=[
            jax.ShapeDtypeStruct((T // seq, ATTN_RESIDUES, seq // ATTN_RESIDUES,
                                  ATT_OPS_WIDTH), BF16),
            jax.ShapeDtypeStruct((T, HG_WIDTH), BF16),
        ],
        scratch_shapes=[pltpu.VMEM((QKV_WIDTH // LANES, tm, LANES), F32)] * 2,
        compiler_params=pltpu.CompilerParams(
            dimension_semantics=("arbitrary",), vmem_limit_bytes=VMEM_LIMIT),
        name="inproj",
    )(x2, g, w)


def _attn_runs(d):
    g = ATTN_RESIDUES // d
    return g, ATTN_BLOCK // g


def _attn_bias():
    blk = ATTN_BLOCK
    e = jnp.arange(2 * blk) % blk
    half = jnp.arange(2 * blk) // blk
    out = []
    for d in DILATIONS:
        g, run = _attn_runs(d)
        pos = g * (e % run) + e // run
        pos_q = pos[:, None]
        pos_k = (blk * half + pos)[None, :]
        dist = blk + pos_q - pos_k
        out.append((dist >= 0) & (dist <= blk))
        out.append(pos_k <= pos_q)
    return jnp.where(jnp.stack(out), 0.0, -jnp.inf).astype(F32)


def _attn_kernel(q0_ref, q1_ref, k_ref, v0_ref, v1_ref, bias_ref, o_ref,
                 m_sc, l_sc, acc_sc, *, seq):
    blk = ATTN_BLOCK
    n_blocks = seq // blk
    ns = seq // ATTN_RESIDUES
    head0 = lax.broadcasted_iota(jnp.int32, (blk, LANES), 1) < ATTN_HEAD_DIM

    def gather(ref, runs):
        return jnp.concatenate([ref[rs, :] for rs in runs], axis=0)

    def block_runs(d, j, which):
        nb = n_blocks // d
        g, run = _attn_runs(d)
        r, n = divmod(j, nb)
        n0 = max(n - 1, 0)
        nsel = {"q": [n], "kv": [n0, n0 + 1]}[which]
        return [pl.ds((d * jj + r) * ns + run * nn, run) for nn in nsel for jj in range(g)]

    def scores_of(group):
        out = []
        for d, j in group:
            qruns = block_runs(d, j, "q")
            qs = jnp.concatenate([gather(q0_ref, qruns),
                                  gather(q1_ref, qruns)], axis=0)
            kw = gather(k_ref, block_runs(d, j, "kv"))
            out.append(lax.dot_general(qs, kw, (((1,), (1,)), ((), ())),
                                       preferred_element_type=F32))
        return out

    def finish(group, scores):
        parts = []
        for (d, j), s in zip(group, scores):
            kruns = block_runs(d, j, "kv")
            first = 1 if j % (n_blocks // d) == 0 else 0
            s = s + bias_ref[2 * DILATIONS.index(d) + first]
            m = jnp.max(s, axis=-1, keepdims=True)
            p = jnp.exp2(s - m).astype(BF16)
            pv0 = jnp.dot(p[:blk], gather(v0_ref, kruns),
                          preferred_element_type=F32)
            pv1 = jnp.dot(p[blk:], gather(v1_ref, kruns),
                          preferred_element_type=F32)
            parts.append((pv0, pv1, m))
        for (d, j), (pv0, pv1, m) in zip(group, parts):
            bi = DILATIONS.index(d)
            qruns = block_runs(d, j, "q")
            o_b = jnp.where(head0, pv0, pv1)
            l_b = jnp.where(head0, pv1, pv0)
            m_b = jnp.where(head0, jnp.broadcast_to(m[:blk], (blk, LANES)),
                            jnp.broadcast_to(m[blk:], (blk, LANES)))
            run_len = blk // len(qruns)
            for i, rs in enumerate(qruns):
                piece = slice(i * run_len, (i + 1) * run_len)
                acc_sc[bi, rs, :] = o_b[piece]
                m_sc[bi, rs, :] = m_b[piece]
                l_sc[bi, rs, :] = l_b[piece]

    blocks = [(d, j) for d in DILATIONS for j in range(n_blocks)]
    groups = [blocks[i:i + ATTN_GROUP] for i in range(0, len(blocks), ATTN_GROUP)]
    scores = scores_of(groups[0])
    for i, group in enumerate(groups):
        nxt = scores_of(groups[i + 1]) if i + 1 < len(groups) else None
        finish(group, scores)
        scores = nxt

    def merge_body(t, carry):
        rows = pl.ds(pl.multiple_of(t * blk, blk), blk)
        ms = [m_sc[bi, rows, :] for bi in range(len(DILATIONS))]
        m_all = functools.reduce(jnp.maximum, ms)
        den = jnp.zeros((blk, LANES), F32)
        num = jnp.zeros((blk, LANES), F32)
        for bi in range(len(DILATIONS)):
            w = jnp.exp2(ms[bi] - m_all)
            den = den + w * pltpu.roll(l_sc[bi, rows, :], ATTN_HEAD_DIM, axis=1)
            num = num + w * acc_sc[bi, rows, :]
        o_ref[rows, :] = num / den
        return carry

    lax.fori_loop(0, n_blocks, merge_body, 0, unroll=16)


def _attention(att_ops, bias):
    B, S, _ = att_ops.shape
    n_pairs = ATTN_WIDTH // LANES
    blockspec = lambda group: pl.BlockSpec((None, S, LANES),
                                           lambda b, p: (b, 0, group * n_pairs + p))
    return pl.pallas_call(
        functools.partial(_attn_kernel, seq=S),
        grid=(B, n_pairs),
        in_specs=[blockspec(group) for group in range(ATT_OPS_WIDTH // ATTN_WIDTH)] + [
            pl.BlockSpec(bias.shape, lambda b, p: (0, 0, 0), pipeline_mode=pl.Buffered(1)),
        ],
        out_specs=pl.BlockSpec((None, S, LANES), lambda b, p: (b, 0, p)),
        out_shape=jax.ShapeDtypeStruct((B, S, ATTN_WIDTH), F32),
        scratch_shapes=[pltpu.VMEM((len(DILATIONS), S, LANES), F32)] * 3,
        compiler_params=pltpu.CompilerParams(
            dimension_semantics=("arbitrary", "arbitrary"),
            vmem_limit_bytes=VMEM_LIMIT),
        name="dilated_attn",
    )(*([att_ops] * (ATT_OPS_WIDTH // ATTN_WIDTH)), bias)


def _hgrn_kernel(q_ref, f_ref, i_ref, g_ref, lbl_ref, gn_ref, o_ref, st_sc, tri_sc, *, seq):
    R = HGRN_ROWS
    C = HGRN_CHUNK
    logits = lbl_ref[...]
    e = jnp.exp(logits - jnp.max(logits, axis=0, keepdims=True))
    lb = e[0:1] / jnp.sum(e, axis=0, keepdims=True)
    f_mid = 0.5 * (1.0 + lb)
    f_half = 0.5 * (1.0 - lb)
    gn = gn_ref[...]

    st_sc[...] = jnp.zeros_like(st_sc)
    row = lax.broadcasted_iota(jnp.int32, (R, R), 0)
    col = lax.broadcasted_iota(jnp.int32, (R, R), 1)
    causal = (row >= col) & ((row // C) == (col // C))
    tri_sc[...] = jnp.where(causal, 1.0, 0.0).astype(BF16)

    G = HGRN_GROUP
    NC = R // C

    def body(it, carry):
        rows = [pl.ds(pl.multiple_of((it * G + g) * R, R), R) for g in range(G)]
        ivs, keys, qfs, b2s = {}, {}, {}, {}
        q_decs, k_ends, decays, a_s = {}, {}, {}, {}
        o_intras, u_ts, o_inters = {}, {}, {}
        st = [st_sc[...]]

        def gates_and_prefix_sum(g):
            q = q_ref[rows[g], :].astype(F32)
            f = f_ref[rows[g], :].astype(F32)
            ivs[g] = i_ref[rows[g], :].astype(BF16)
            hq = 0.5 * q
            qfs[g] = hq + hq * jnp.tanh(hq)
            forget = f_mid + f_half * jnp.tanh(0.5 * f)
            keys[g] = 1.0 - forget
            lf = jnp.log2(forget)
            lf_hi = lf.astype(BF16)
            lf_lo = (lf - lf_hi.astype(F32)).astype(BF16)
            b2s[g] = jnp.dot(tri_sc[...], jnp.concatenate([lf_hi, lf_lo], axis=1),
                             preferred_element_type=F32)

        def intra_chunk_scores(g):
            b2 = b2s.pop(g)
            b = b2[:, :HGRN_DIM] + b2[:, HGRN_DIM:]
            decay_rows = [jnp.exp2(b[j * C + C - 1:j * C + C]) for j in range(NC)]
            decay_end = jnp.concatenate(
                [jnp.broadcast_to(d, (C, HGRN_DIM)) for d in decay_rows], axis=0)
            k_inv_f = keys.pop(g) * jnp.exp2(-b)
            q_decs[g] = (qfs.pop(g) * jnp.exp2(b)).astype(BF16)
            k_ends[g] = (k_inv_f * decay_end).astype(BF16)
            decays[g] = decay_rows
            a_s[g] = lax.dot_general(q_decs[g], k_inv_f.astype(BF16),
                                     (((1,), (1,)), ((), ())),
                                     preferred_element_type=F32)

        def intra_chunk_outputs(g):
            a = jnp.where(causal, a_s.pop(g), 0.0).astype(BF16)
            iv = ivs.pop(g)
            k_end = k_ends.pop(g)
            o_intras[g] = jnp.dot(a, iv, preferred_element_type=F32)
            u_ts[g] = [lax.dot_general(iv[j * C:(j + 1) * C], k_end[j * C:(j + 1) * C],
                                       (((0,), (0,)), ((), ())),
                                       preferred_element_type=F32)
                       for j in range(NC)]

        def state_scan(g):
            parts = []
            q_dec = q_decs.pop(g)
            for j in range(NC):
                parts.append(lax.dot_general(q_dec[j * C:(j + 1) * C], st[0].astype(BF16),
                                             (((1,), (1,)), ((), ())),
                                             preferred_element_type=F32))
                st[0] = st[0] * decays[g][j] + u_ts[g][j]
            o_inters[g] = jnp.concatenate(parts, axis=0)

        def emit(g):
            o = _rms(o_intras.pop(g) + o_inters.pop(g), gn)
            hg = 0.5 * g_ref[rows[g], :].astype(F32)
            o_ref[rows[g], :] = (o * (hg + hg * jnp.tanh(hg))).astype(o_ref.dtype)

        stages = (gates_and_prefix_sum, intra_chunk_scores, intra_chunk_outputs,
                  state_scan, emit)
        for t in range(G + len(stages) - 1):
            for k in reversed(range(len(stages))):
                g = t - k
                if 0 <= g < G:
                    stages[k](g)
        st_sc[...] = st[0]
        return carry

    lax.fori_loop(0, seq // (R * G), body, 0)


def _hgrn(hg, lb_logits, gn):
    B, S, _ = hg.shape
    H = HGRN_HEADS
    n_rows = lb_logits.shape[1]
    blockspec = lambda off: pl.BlockSpec((None, S, HGRN_DIM), lambda b, h: (b, 0, h + off))
    return pl.pallas_call(
        functools.partial(_hgrn_kernel, seq=S),
        grid=(B, H),
        in_specs=[
            blockspec(0), blockspec(H), blockspec(2 * H), blockspec(3 * H),
            pl.BlockSpec((None, n_rows, HGRN_DIM), lambda b, h: (h, 0, 0)),
            pl.BlockSpec((None, 1, HGRN_DIM), lambda b, h: (h, 0, 0)),
        ],
        out_specs=pl.BlockSpec((None, S, HGRN_DIM), lambda b, h: (b, 0, h)),
        out_shape=jax.ShapeDtypeStruct((B, S, HGRN_WIDTH), BF16),
        scratch_shapes=[pltpu.VMEM((HGRN_DIM, HGRN_DIM), F32),
                        pltpu.VMEM((HGRN_ROWS, HGRN_ROWS), BF16)],
        compiler_params=pltpu.CompilerParams(
            dimension_semantics=("arbitrary", "arbitrary"),
            vmem_limit_bytes=VMEM_LIMIT),
        name="hgrn2",
    )(hg, hg, hg, hg, lb_logits, gn)


def _ffn_kernel(x_ref, attn_ref, rec_ref, ga_ref, wout_ref, g2_ref, wup_ref,
                cw_ref, cb_ref, wdown_ref, gf_ref, o_ref,
                h_sc, u_sc, act_sc, carry_sc, nat_sc, nat2_sc, *, tiles_per_seq):
    tm = x_ref.shape[0]
    i = pl.program_id(0)

    @pl.when(i % tiles_per_seq == 0)
    def _():
        carry_sc[...] = jnp.zeros_like(carry_sc)

    an = _rms(attn_ref[...], ga_ref[...])
    run = tm // ATTN_RESIDUES
    quarter = tm // 4
    for s in range(ATTN_WIDTH // LANES):
        for r16 in range(ATTN_RESIDUES):
            nat_sc[s, pl.ds((r16 % 4) * quarter + r16 // 4, run, stride=4), :] = (
                an[r16, :, s * LANES:(s + 1) * LANES])
        for r4 in range(4):
            nat2_sc[s, pl.ds(r4, quarter, stride=4), :] = nat_sc[s, r4 * quarter:(r4 + 1) * quarter, :]
    an = jnp.concatenate([nat2_sc[s] for s in range(ATTN_WIDTH // LANES)], axis=1)
    mix = jnp.dot(an.astype(BF16), wout_ref[:ATTN_WIDTH, :], preferred_element_type=F32)
    mix = mix + jnp.dot(rec_ref[...], wout_ref[ATTN_WIDTH:, :],
                        preferred_element_type=F32)
    h = x_ref[...] + mix
    h_sc[...] = h
    u_sc[...] = _rms(h, g2_ref[...]).astype(BF16)

    sqrt_half = math.sqrt(0.5)

    def chunk(start, width):
        u = u_sc[...]
        row = lax.broadcasted_iota(jnp.int32, (tm, width), 0)
        cols = slice(start, start + width)
        vcols = slice(D_FF + start, D_FF + start + width)
        gate = jnp.dot(u, wup_ref[:, cols], preferred_element_type=F32)
        val = jnp.dot(u, wup_ref[:, vcols], preferred_element_type=F32)
        prev = carry_sc[:, cols]
        carry_sc[:, cols] = gate[tm - SUBLANES:, :]
        p1 = jnp.broadcast_to(prev[SUBLANES - 1:SUBLANES], (tm, width))
        p2 = jnp.broadcast_to(prev[SUBLANES - 2:SUBLANES - 1], (tm, width))
        g1 = jnp.where(row == 0, p1, pltpu.roll(gate, 1, axis=0))
        g2 = jnp.where(row == 0, p2, jnp.where(row == 1, p1,
                                               pltpu.roll(gate, 2, axis=0)))
        cw = cw_ref[:, cols]
        conv = cb_ref[:, cols] + cw[0:1] * g2 + cw[1:2] * g1 + cw[2:3] * gate
        act = 0.5 * conv * (1.0 + lax.erf(conv * sqrt_half)) * val
        act_sc[:, cols] = act.astype(BF16)

    start = 0
    for width in FF_CHUNKS:
        chunk(start, width)
        start += width
    down = jnp.dot(act_sc[...], wdown_ref[...], preferred_element_type=F32)
    o_ref[...] = _rms(h_sc[...] + down, gf_ref[...])


def _ffn(x2, attn, rec2, ga, wout, g2, wup, cw, cb, wdown, gf, tm, seq):
    T = x2.shape[0]
    tps = seq // tm
    const = lambda shape: pl.BlockSpec(shape, lambda i: (0,) * len(shape),
                                       pipeline_mode=pl.Buffered(1))
    return pl.pallas_call(
        functools.partial(_ffn_kernel, tiles_per_seq=tps),
        grid=(T // tm,),
        in_specs=[
            pl.BlockSpec((tm, D_MODEL), lambda i: (i, 0)),
            pl.BlockSpec((None, ATTN_RESIDUES, tm // ATTN_RESIDUES, ATTN_WIDTH),
                         lambda i: (i // tps, 0, i % tps, 0)),
            pl.BlockSpec((tm, HGRN_WIDTH), lambda i: (i, 0)),
            const((1, ATTN_WIDTH)),
            const((D_MODEL, D_MODEL)),
            const((1, D_MODEL)),
            const((D_MODEL, 2 * D_FF)),
            const((CONV_WIDTH, D_FF)),
            const((1, D_FF)),
            const((D_FF, D_MODEL)),
            const((1, D_MODEL)),
        ],
        out_specs=pl.BlockSpec((tm, D_MODEL), lambda i: (i, 0)),
        out_shape=jax.ShapeDtypeStruct((T, D_MODEL), F32),
        scratch_shapes=[
            pltpu.VMEM((tm, D_MODEL), F32),
            pltpu.VMEM((tm, D_MODEL), BF16),
            pltpu.VMEM((tm, D_FF), BF16),
            pltpu.VMEM((SUBLANES, D_FF), F32),
            pltpu.VMEM((ATTN_WIDTH // LANES, tm, LANES), F32),
            pltpu.VMEM((ATTN_WIDTH // LANES, tm, LANES), F32),
        ],
        compiler_params=pltpu.CompilerParams(
            dimension_semantics=("arbitrary",), vmem_limit_bytes=VMEM_LIMIT),
        name="outproj_convglu",
    )(x2, attn, rec2, ga, wout, g2, wup, cw, cb, wdown, gf)


def kernel(x, norm1_g, w_in, attn_norm_g, hgrn_norm_g, hgrn_lb_logits, w_out,
           norm2_g, w_up, conv_w, conv_b, w_down, final_norm_g):
    B, S, D = x.shape
    depth = w_in.shape[0]
    assert depth == 1 and D == D_MODEL
    assert S % (ATTN_BLOCK * DILATIONS[-1]) == 0 and S % (HGRN_ROWS * HGRN_GROUP) == 0
    assert S % PROJ_TILE == 0 and S % FFN_TILE == 0
    T = B * S
    layer = 0
    x2 = x.reshape(T, D)

    att_ops, hg = _inproj(x2, norm1_g[layer].reshape(1, D), w_in[layer].astype(BF16),
                          PROJ_TILE, S)

    attn = _attention(att_ops.reshape(B, S, ATT_OPS_WIDTH), _attn_bias())
    attn = attn.reshape(B, ATTN_RESIDUES, S // ATTN_RESIDUES, ATTN_WIDTH)

    lbl = hgrn_lb_logits.astype(F32).reshape(depth + 1, HGRN_HEADS, HGRN_DIM)
    lbl = lbl.transpose(1, 0, 2)
    gn = hgrn_norm_g[layer].reshape(HGRN_HEADS, 1, HGRN_DIM)
    rec = _hgrn(hg.reshape(B, S, HG_WIDTH), lbl, gn)

    wup = w_up[layer].astype(BF16)
    cw = conv_w[layer]
    cb = conv_b[layer].reshape(1, D_FF)
    wdown = w_down[layer].astype(BF16)
    out = _ffn(x2, attn, rec.reshape(T, HGRN_WIDTH),
               attn_norm_g[layer].reshape(1, ATTN_WIDTH), w_out[layer].astype(BF16),
               norm2_g[layer].reshape(1, D), wup, cw, cb, wdown,
               final_norm_g.reshape(1, D), FFN_TILE, S)
    return out.reshape(B, S, D)
```

```python
import functools
import math

import jax
import jax.numpy as jnp
from jax import lax
from jax.experimental import pallas as pl
from jax.experimental.pallas import tpu as pltpu

F32 = jnp.float32
BF16 = jnp.bfloat16

D_MODEL = 1024
ATTN_WIDTH = 512
ATTN_HEADS = 8
ATTN_HEAD_DIM = 64
ATTN_BLOCK = 128
DILATIONS = (1, 4, 16)
ATTN_RESIDUES = 16
HGRN_WIDTH = 512
HGRN_HEADS = 4
HGRN_DIM = 128
HGRN_CHUNK = 64
D_FF = 2816
CONV_WIDTH = 3
NORM_EPS = 1e-6
LOG2_E = math.log2(math.e)
QKV_WIDTH = 3 * ATTN_WIDTH
ATT_OPS_WIDTH = 5 * ATTN_WIDTH
HG_WIDTH = 4 * HGRN_WIDTH

LANES = 128
SUBLANES = 8
MXU_COLS = 256
FF_CHUNKS = (768, 768, 768, 512)
PROJ_TILE = 1024
FFN_TILE = 512
HGRN_ROWS = 256
HGRN_GROUP = 16
ATTN_GROUP = 16
VMEM_LIMIT = 56 * 1024 * 1024


def _rms(x, g):
    return x * lax.rsqrt(jnp.mean(x * x, axis=-1, keepdims=True) + NORM_EPS) * g


def _inproj_kernel(x_ref, g_ref, w_ref, att_ref, hg_ref, res_sc, tmp_sc):
    tm = x_ref.shape[0]
    run = tm // ATTN_RESIDUES
    quarter = tm // 4
    u = _rms(x_ref[...], g_ref[...]).astype(BF16)
    n_att = QKV_WIDTH // MXU_COLS
    n_gate = HG_WIDTH // MXU_COLS
    for c in range(max(n_att, n_gate)):
        if c < n_att:
            r = jnp.dot(u, w_ref[:, c * MXU_COLS:(c + 1) * MXU_COLS],
                        preferred_element_type=F32)
            res_sc[2 * c] = r[:, :LANES]
            res_sc[2 * c + 1] = r[:, LANES:]
        if c < n_gate:
            cols = slice(c * MXU_COLS, (c + 1) * MXU_COLS)
            wcols = slice(QKV_WIDTH + c * MXU_COLS, QKV_WIDTH + (c + 1) * MXU_COLS)
            hg_ref[:, cols] = jnp.dot(u, w_ref[:, wcols],
                                      preferred_element_type=F32).astype(hg_ref.dtype)
    head0 = lax.broadcasted_iota(jnp.int32, (run, LANES), 1) < ATTN_HEAD_DIM
    scale = ATTN_HEAD_DIM ** -0.5 * LOG2_E
    zero = jnp.zeros((run, LANES), F32)
    one = jnp.ones((run, LANES), F32)
    pairs = ATTN_WIDTH // LANES
    for s in range(QKV_WIDTH // LANES):
        kind, p = divmod(s, pairs)
        cols = lambda group: slice((group * pairs + p) * LANES, (group * pairs + p + 1) * LANES)
        for r4 in range(4):
            tmp_sc[s, r4 * quarter:(r4 + 1) * quarter, :] = res_sc[s, pl.ds(r4, quarter, stride=4), :]
        for r16 in range(ATTN_RESIDUES):
            rows = tmp_sc[s, pl.ds((r16 % 4) * quarter + r16 // 4, run, stride=4), :]
            if kind == 0:
                q = rows * scale
                att_ref[r16, :, cols(0)] = jnp.where(head0, q, zero).astype(BF16)
                att_ref[r16, :, cols(1)] = jnp.where(head0, zero, q).astype(BF16)
            elif kind == 1:
                att_ref[r16, :, cols(2)] = rows.astype(BF16)
            else:
                att_ref[r16, :, cols(3)] = jnp.where(head0, rows, one).astype(BF16)
                att_ref[r16, :, cols(4)] = jnp.where(head0, one, rows).astype(BF16)


def _inproj(x2, g, w, tm, seq):
    T = x2.shape[0]
    tps = seq // tm
    run = tm // ATTN_RESIDUES
    return pl.pallas_call(
        _inproj_kernel,
        grid=(T // tm,),
        in_specs=[
            pl.BlockSpec((tm, D_MODEL), lambda i: (i, 0)),
            pl.BlockSpec((1, D_MODEL), lambda i: (0, 0)),
            pl.BlockSpec((D_MODEL, QKV_WIDTH + HG_WIDTH), lambda i: (0, 0),
                         pipeline_mode=pl.Buffered(1)),
        ],
        out_specs=[
            pl.BlockSpec((None, ATTN_RESIDUES, run, ATT_OPS_WIDTH),
                         lambda i: (i // tps, 0, i % tps, 0)),
            pl.BlockSpec((tm, HG_WIDTH), lambda i: (i, 0)),
        ],
        out_shape=[
            jax.ShapeDtypeStruct((T // seq, ATTN_RESIDUES, seq // ATTN_RESIDUES,
                                  ATT_OPS_WIDTH), BF16),
            jax.ShapeDtypeStruct((T, HG_WIDTH), BF16),
        ],
        scratch_shapes=[pltpu.VMEM((QKV_WIDTH // LANES, tm, LANES), F32)] * 2,
        compiler_params=pltpu.CompilerParams(
            dimension_semantics=("arbitrary",), vmem_limit_bytes=VMEM_LIMIT),
        name="inproj",
    )(x2, g, w)


def _attn_runs(d):
    g = ATTN_RESIDUES // d
    return g, ATTN_BLOCK // g


def _attn_bias():
    blk = ATTN_BLOCK
    e = jnp.arange(2 * blk) % blk
    half = jnp.arange(2 * blk) // blk
    out = []
    for d in DILATIONS:
        g, run = _attn_runs(d)
        pos = g * (e % run) + e // run
        pos_q = pos[:, None]
        pos_k = (blk * half + pos)[None, :]
        dist = blk + pos_q - pos_k
        out.append((dist >= 0) & (dist <= blk))
        out.append(pos_k <= pos_q)
    return jnp.where(jnp.stack(out), 0.0, -jnp.inf).astype(F32)


def _attn_kernel(q0_ref, q1_ref, k_ref, v0_ref, v1_ref, bias_ref, o_ref,
                 m_sc, l_sc, acc_sc, *, seq):
    blk = ATTN_BLOCK
    n_blocks = seq // blk
    ns = seq // ATTN_RESIDUES
    head0 = lax.broadcasted_iota(jnp.int32, (blk, LANES), 1) < ATTN_HEAD_DIM

    def gather(ref, runs):
        return jnp.concatenate([ref[rs, :] for rs in runs], axis=0)

    def block_runs(d, j, which):
        nb = n_blocks // d
        g, run = _attn_runs(d)
        r, n = divmod(j, nb)
        n0 = max(n - 1, 0)
        nsel = {"q": [n], "kv": [n0, n0 + 1]}[which]
        return [pl.ds((d * jj + r) * ns + run * nn, run) for nn in nsel for jj in range(g)]

    def scores_of(group):
        out = []
        for d, j in group:
            qruns = block_runs(d, j, "q")
            qs = jnp.concatenate([gather(q0_ref, qruns),
                                  gather(q1_ref, qruns)], axis=0)
            kw = gather(k_ref, block_runs(d, j, "kv"))
            out.append(lax.dot_general(qs, kw, (((1,), (1,)), ((), ())),
                                       preferred_element_type=F32))
        return out

    def finish(group, scores):
        parts = []
        for (d, j), s in zip(group, scores):
            kruns = block_runs(d, j, "kv")
            first = 1 if j % (n_blocks // d) == 0 else 0
            s = s + bias_ref[2 * DILATIONS.index(d) + first]
            m = jnp.max(s, axis=-1, keepdims=True)
            p = jnp.exp2(s - m).astype(BF16)
            pv0 = jnp.dot(p[:blk], gather(v0_ref, kruns),
                          preferred_element_type=F32)
            pv1 = jnp.dot(p[blk:], gather(v1_ref, kruns),
                          preferred_element_type=F32)
            parts.append((pv0, pv1, m))
        for (d, j), (pv0, pv1, m) in zip(group, parts):
            bi = DILATIONS.index(d)
            qruns = block_runs(d, j, "q")
            o_b = jnp.where(head0, pv0, pv1)
            l_b = jnp.where(head0, pv1, pv0)
            m_b = jnp.where(head0, jnp.broadcast_to(m[:blk], (blk, LANES)),
                            jnp.broadcast_to(m[blk:], (blk, LANES)))
            run_len = blk // len(qruns)
            for i, rs in enumerate(qruns):
                piece = slice(i * run_len, (i + 1) * run_len)
                acc_sc[bi, rs, :] = o_b[piece]
                m_sc[bi, rs, :] = m_b[piece]
                l_sc[bi, rs, :] = l_b[piece]

    blocks = [(d, j) for d in DILATIONS for j in range(n_blocks)]
    groups = [blocks[i:i + ATTN_GROUP] for i in range(0, len(blocks), ATTN_GROUP)]
    scores = scores_of(groups[0])
    for i, group in enumerate(groups):
        nxt = scores_of(groups[i + 1]) if i + 1 < len(groups) else None
        finish(group, scores)
        scores = nxt

    def merge_body(t, carry):
        rows = pl.ds(pl.multiple_of(t * blk, blk), blk)
        ms = [m_sc[bi, rows, :] for bi in range(len(DILATIONS))]
        m_all = functools.reduce(jnp.maximum, ms)
        den = jnp.zeros((blk, LANES), F32)
        num = jnp.zeros((blk, LANES), F32)
        for bi in range(len(DILATIONS)):
            w = jnp.exp2(ms[bi] - m_all)
            den = den + w * pltpu.roll(l_sc[bi, rows, :], ATTN_HEAD_DIM, axis=1)
            num = num + w * acc_sc[bi, rows, :]
        o_ref[rows, :] = num / den
        return carry

    lax.fori_loop(0, n_blocks, merge_body, 0, unroll=16)


def _attention(att_ops, bias):
    B, S, _ = att_ops.shape
    n_pairs = ATTN_WIDTH // LANES
    blockspec = lambda group: pl.BlockSpec((None, S, LANES),
                                           lambda b, p: (b, 0, group * n_pairs + p))
    return pl.pallas_call(
        functools.partial(_attn_kernel, seq=S),
        grid=(B, n_pairs),
        in_specs=[blockspec(group) for group in range(ATT_OPS_WIDTH // ATTN_WIDTH)] + [
            pl.BlockSpec(bias.shape, lambda b, p: (0, 0, 0), pipeline_mode=pl.Buffered(1)),
        ],
        out_specs=pl.BlockSpec((None, S, LANES), lambda b, p: (b, 0, p)),
        out_shape=jax.ShapeDtypeStruct((B, S, ATTN_WIDTH), F32),
        scratch_shapes=[pltpu.VMEM((len(DILATIONS), S, LANES), F32)] * 3,
        compiler_params=pltpu.CompilerParams(
            dimension_semantics=("arbitrary", "arbitrary"),
            vmem_limit_bytes=VMEM_LIMIT),
        name="dilated_attn",
    )(*([att_ops] * (ATT_OPS_WIDTH // ATTN_WIDTH)), bias)


def _hgrn_kernel(q_ref, f_ref, i_ref, g_ref, lbl_ref, gn_ref, o_ref, st_sc, tri_sc, *, seq):
    R = HGRN_ROWS
    C = HGRN_CHUNK
    logits = lbl_ref[...]
    e = jnp.exp(logits - jnp.max(logits, axis=0, keepdims=True))
    lb = e[0:1] / jnp.sum(e, axis=0, keepdims=True)
    f_mid = 0.5 * (1.0 + lb)
    f_half = 0.5 * (1.0 - lb)
    gn = gn_ref[...]

    st_sc[...] = jnp.zeros_like(st_sc)
    row = lax.broadcasted_iota(jnp.int32, (R, R), 0)
    col = lax.broadcasted_iota(jnp.int32, (R, R), 1)
    causal = (row >= col) & ((row // C) == (col // C))
    tri_sc[...] = jnp.where(causal, 1.0, 0.0).astype(BF16)

    G = HGRN_GROUP
    NC = R // C

    def body(it, carry):
        rows = [pl.ds(pl.multiple_of((it * G + g) * R, R), R) for g in range(G)]
        ivs, keys, qfs, b2s = {}, {}, {}, {}
        q_decs, k_ends, decays, a_s = {}, {}, {}, {}
        o_intras, u_ts, o_inters = {}, {}, {}
        st = [st_sc[...]]

        def gates_and_prefix_sum(g):
            q = q_ref[rows[g], :].astype(F32)
            f = f_ref[rows[g], :].astype(F32)
            ivs[g] = i_ref[rows[g], :].astype(BF16)
            hq = 0.5 * q
            qfs[g] = hq + hq * jnp.tanh(hq)
            forget = f_mid + f_half * jnp.tanh(0.5 * f)
            keys[g] = 1.0 - forget
            lf = jnp.log2(forget)
            lf_hi = lf.astype(BF16)
            lf_lo = (lf - lf_hi.astype(F32)).astype(BF16)
            b2s[g] = jnp.dot(tri_sc[...], jnp.concatenate([lf_hi, lf_lo], axis=1),
                             preferred_element_type=F32)

        def intra_chunk_scores(g):
            b2 = b2s.pop(g)
            b = b2[:, :HGRN_DIM] + b2[:, HGRN_DIM:]
            decay_rows = [jnp.exp2(b[j * C + C - 1:j * C + C]) for j in range(NC)]
            decay_end = jnp.concatenate(
                [jnp.broadcast_to(d, (C, HGRN_DIM)) for d in decay_rows], axis=0)
            k_inv_f = keys.pop(g) * jnp.exp2(-b)
            q_decs[g] = (qfs.pop(g) * jnp.exp2(b)).astype(BF16)
            k_ends[g] = (k_inv_f * decay_end).astype(BF16)
            decays[g] = decay_rows
            a_s[g] = lax.dot_general(q_decs[g], k_inv_f.astype(BF16),
                                     (((1,), (1,)), ((), ())),
                                     preferred_element_type=F32)

        def intra_chunk_outputs(g):
            a = jnp.where(causal, a_s.pop(g), 0.0).astype(BF16)
            iv = ivs.pop(g)
            k_end = k_ends.pop(g)
            o_intras[g] = jnp.dot(a, iv, preferred_element_type=F32)
            u_ts[g] = [lax.dot_general(iv[j * C:(j + 1) * C], k_end[j * C:(j + 1) * C],
                                       (((0,), (0,)), ((), ())),
                                       preferred_element_type=F32)
                       for j in range(NC)]

        def state_scan(g):
            parts = []
            q_dec = q_decs.pop(g)
            for j in range(NC):
                parts.append(lax.dot_general(q_dec[j * C:(j + 1) * C], st[0].astype(BF16),
                                             (((1,), (1,)), ((), ())),
                                             preferred_element_type=F32))
                st[0] = st[0] * decays[g][j] + u_ts[g][j]
            o_inters[g] = jnp.concatenate(parts, axis=0)

        def emit(g):
            o = _rms(o_intras.pop(g) + o_inters.pop(g), gn)
            hg = 0.5 * g_ref[rows[g], :].astype(F32)
            o_ref[rows[g], :] = (o * (hg + hg * jnp.tanh(hg))).astype(o_ref.dtype)

        stages = (gates_and_prefix_sum, intra_chunk_scores, intra_chunk_outputs,
                  state_scan, emit)
        for t in range(G + len(stages) - 1):
            for k in reversed(range(len(stages))):
                g = t - k
                if 0 <= g < G:
                    stages[k](g)
        st_sc[...] = st[0]
        return carry

    lax.fori_loop(0, seq // (R * G), body, 0)


def _hgrn(hg, lb_logits, gn):
    B, S, _ = hg.shape
    H = HGRN_HEADS
    n_rows = lb_logits.shape[1]
    blockspec = lambda off: pl.BlockSpec((None, S, HGRN_DIM), lambda b, h: (b, 0, h + off))
    return pl.pallas_call(
        functools.partial(_hgrn_kernel, seq=S),
        grid=(B, H),
        in_specs=[
            blockspec(0), blockspec(H), blockspec(2 * H), blockspec(3 * H),
            pl.BlockSpec((None, n_rows, HGRN_DIM), lambda b, h: (h, 0, 0)),
            pl.BlockSpec((None, 1, HGRN_DIM), lambda b, h: (h, 0, 0)),
        ],
        out_specs=pl.BlockSpec((None, S, HGRN_DIM), lambda b, h: (b, 0, h)),
        out_shape=jax.ShapeDtypeStruct((B, S, HGRN_WIDTH), BF16),
        scratch_shapes=[pltpu.VMEM((HGRN_DIM, HGRN_DIM), F32),
                        pltpu.VMEM((HGRN_ROWS, HGRN_ROWS), BF16)],
        compiler_params=pltpu.CompilerParams(
            dimension_semantics=("arbitrary", "arbitrary"),
            vmem_limit_bytes=VMEM_LIMIT),
        name="hgrn2",
    )(hg, hg, hg, hg, lb_logits, gn)


def _ffn_kernel(x_ref, attn_ref, rec_ref, ga_ref, wout_ref, g2_ref, wup_ref,
                cw_ref, cb_ref, wdown_ref, gf_ref, o_ref,
                h_sc, u_sc, act_sc, carry_sc, nat_sc, nat2_sc, *, tiles_per_seq):
    tm = x_ref.shape[0]
    i = pl.program_id(0)

    @pl.when(i % tiles_per_seq == 0)
    def _():
        carry_sc[...] = jnp.zeros_like(carry_sc)

    an = _rms(attn_ref[...], ga_ref[...])
    run = tm // ATTN_RESIDUES
    quarter = tm // 4
    for s in range(ATTN_WIDTH // LANES):
        for r16 in range(ATTN_RESIDUES):
            nat_sc[s, pl.ds((r16 % 4) * quarter + r16 // 4, run, stride=4), :] = (
                an[r16, :, s * LANES:(s + 1) * LANES])
        for r4 in range(4):
            nat2_sc[s, pl.ds(r4, quarter, stride=4), :] = nat_sc[s, r4 * quarter:(r4 + 1) * quarter, :]
    an = jnp.concatenate([nat2_sc[s] for s in range(ATTN_WIDTH // LANES)], axis=1)
    mix = jnp.dot(an.astype(BF16), wout_ref[:ATTN_WIDTH, :], preferred_element_type=F32)
    mix = mix + jnp.dot(rec_ref[...], wout_ref[ATTN_WIDTH:, :],
                        preferred_element_type=F32)
    h = x_ref[...] + mix
    h_sc[...] = h
    u_sc[...] = _rms(h, g2_ref[...]).astype(BF16)

    sqrt_half = math.sqrt(0.5)

    def chunk(start, width):
        u = u_sc[...]
        row = lax.broadcasted_iota(jnp.int32, (tm, width), 0)
        cols = slice(start, start + width)
        vcols = slice(D_FF + start, D_FF + start + width)
        gate = jnp.dot(u, wup_ref[:, cols], preferred_element_type=F32)
        val = jnp.dot(u, wup_ref[:, vcols], preferred_element_type=F32)
        prev = carry_sc[:, cols]
        carry_sc[:, cols] = gate[tm - SUBLANES:, :]
        p1 = jnp.broadcast_to(prev[SUBLANES - 1:SUBLANES], (tm, width))
        p2 = jnp.broadcast_to(prev[SUBLANES - 2:SUBLANES - 1], (tm, width))
        g1 = jnp.where(row == 0, p1, pltpu.roll(gate, 1, axis=0))
        g2 = jnp.where(row == 0, p2, jnp.where(row == 1, p1,
                                               pltpu.roll(gate, 2, axis=0)))
        cw = cw_ref[:, cols]
        conv = cb_ref[:, cols] + cw[0:1] * g2 + cw[1:2] * g1 + cw[2:3] * gate
        act = 0.5 * conv * (1.0 + lax.erf(conv * sqrt_half)) * val
        act_sc[:, cols] = act.astype(BF16)

    start = 0
    for width in FF_CHUNKS:
        chunk(start, width)
        start += width
    down = jnp.dot(act_sc[...], wdown_ref[...], preferred_element_type=F32)
    o_ref[...] = _rms(h_sc[...] + down, gf_ref[...])


def _ffn(x2, attn, rec2, ga, wout, g2, wup, cw, cb, wdown, gf, tm, seq):
    T = x2.shape[0]
    tps = seq // tm
    const = lambda shape: pl.BlockSpec(shape, lambda i: (0,) * len(shape),
                                       pipeline_mode=pl.Buffered(1))
    return pl.pallas_call(
        functools.partial(_ffn_kernel, tiles_per_seq=tps),
        grid=(T // tm,),
        in_specs=[
            pl.BlockSpec((tm, D_MODEL), lambda i: (i, 0)),
            pl.BlockSpec((None, ATTN_RESIDUES, tm // ATTN_RESIDUES, ATTN_WIDTH),
                         lambda i: (i // tps, 0, i % tps, 0)),
            pl.BlockSpec((tm, HGRN_WIDTH), lambda i: (i, 0)),
            const((1, ATTN_WIDTH)),
            const((D_MODEL, D_MODEL)),
            const((1, D_MODEL)),
            const((D_MODEL, 2 * D_FF)),
            const((CONV_WIDTH, D_FF)),
            const((1, D_FF)),
            const((D_FF, D_MODEL)),
            const((1, D_MODEL)),
        ],
        out_specs=pl.BlockSpec((tm, D_MODEL), lambda i: (i, 0)),
        out_shape=jax.ShapeDtypeStruct((T, D_MODEL), F32),
        scratch_shapes=[
            pltpu.VMEM((tm, D_MODEL), F32),
            pltpu.VMEM((tm, D_MODEL), BF16),
            pltpu.VMEM((tm, D_FF), BF16),
            pltpu.VMEM((SUBLANES, D_FF), F32),
            pltpu.VMEM((ATTN_WIDTH // LANES, tm, LANES), F32),
            pltpu.VMEM((ATTN_WIDTH // LANES, tm, LANES), F32),
        ],
        compiler_params=pltpu.CompilerParams(
            dimension_semantics=("arbitrary",), vmem_limit_bytes=VMEM_LIMIT),
        name="outproj_convglu",
    )(x2, attn, rec2, ga, wout, g2, wup, cw, cb, wdown, gf)


def kernel(x, norm1_g, w_in, attn_norm_g, hgrn_norm_g, hgrn_lb_logits, w_out,
           norm2_g, w_up, conv_w, conv_b, w_down, final_norm_g):
    B, S, D = x.shape
    depth = w_in.shape[0]
    assert depth == 1 and D == D_MODEL
    assert S % (ATTN_BLOCK * DILATIONS[-1]) == 0 and S % (HGRN_ROWS * HGRN_GROUP) == 0
    assert S % PROJ_TILE == 0 and S % FFN_TILE == 0
    T = B * S
    layer = 0
    x2 = x.reshape(T, D)

    att_ops, hg = _inproj(x2, norm1_g[layer].reshape(1, D), w_in[layer].astype(BF16),
                          PROJ_TILE, S)

    attn = _attention(att_ops.reshape(B, S, ATT_OPS_WIDTH), _attn_bias())
    attn = attn.reshape(B, ATTN_RESIDUES, S // ATTN_RESIDUES, ATTN_WIDTH)

    lbl = hgrn_lb_logits.astype(F32).reshape(depth + 1, HGRN_HEADS, HGRN_DIM)
    lbl = lbl.transpose(1, 0, 2)
    gn = hgrn_norm_g[layer].reshape(HGRN_HEADS, 1, HGRN_DIM)
    rec = _hgrn(hg.reshape(B, S, HG_WIDTH), lbl, gn)

    wup = w_up[layer].astype(BF16)
    cw = conv_w[layer]
    cb = conv_b[layer].reshape(1, D_FF)
    wdown = w_down[layer].astype(BF16)
    out = _ffn(x2, attn, rec.reshape(T, HGRN_WIDTH),
               attn_norm_g[layer].reshape(1, ATTN_WIDTH), w_out[layer].astype(BF16),
               norm2_g[layer].reshape(1, D), wup, cw, cb, wdown,
               final_norm_g.reshape(1, D), FFN_TILE, S)
    return out.reshape(B, S, D)
```

```python
import functools
import math

import jax
import jax.numpy as jnp
from jax import lax
from jax.experimental import pallas as pl
from jax.experimental.pallas import tpu as pltpu

F32 = jnp.float32
BF16 = jnp.bfloat16

D_MODEL = 1024
ATTN_WIDTH = 512
ATTN_HEADS = 8
ATTN_HEAD_DIM = 64
ATTN_BLOCK = 128
DILATIONS = (1, 4, 16)
ATTN_RESIDUES = 16
HGRN_WIDTH = 512
HGRN_HEADS = 4
HGRN_DIM = 128
HGRN_CHUNK = 64
D_FF = 2816
CONV_WIDTH = 3
NORM_EPS = 1e-6
LOG2_E = math.log2(math.e)
QKV_WIDTH = 3 * ATTN_WIDTH
ATT_OPS_WIDTH = 5 * ATTN_WIDTH
HG_WIDTH = 4 * HGRN_WIDTH

LANES = 128
SUBLANES = 8
MXU_COLS = 256
FF_CHUNKS = (768, 768, 768, 512)
PROJ_TILE = 1024
FFN_TILE = 512
HGRN_ROWS = 256
HGRN_GROUP = 16
ATTN_GROUP = 16
VMEM_LIMIT = 56 * 1024 * 1024


def _rms(x, g):
    return x * lax.rsqrt(jnp.mean(x * x, axis=-1, keepdims=True) + NORM_EPS) * g


def _inproj_kernel(x_ref, g_ref, w_ref, att_ref, hg_ref, res_sc, tmp_sc):
    tm = x_ref.shape[0]
    run = tm // ATTN_RESIDUES
    quarter = tm // 4
    u = _rms(x_ref[...], g_ref[...]).astype(BF16)
    n_att = QKV_WIDTH // MXU_COLS
    n_gate = HG_WIDTH // MXU_COLS
    for c in range(max(n_att, n_gate)):
        if c < n_att:
            r = jnp.dot(u, w_ref[:, c * MXU_COLS:(c + 1) * MXU_COLS],
                        preferred_element_type=F32)
            res_sc[2 * c] = r[:, :LANES]
            res_sc[2 * c + 1] = r[:, LANES:]
        if c < n_gate:
            cols = slice(c * MXU_COLS, (c + 1) * MXU_COLS)
            wcols = slice(QKV_WIDTH + c * MXU_COLS, QKV_WIDTH + (c + 1) * MXU_COLS)
            hg_ref[:, cols] = jnp.dot(u, w_ref[:, wcols],
                                      preferred_element_type=F32).astype(hg_ref.dtype)
    head0 = lax.broadcasted_iota(jnp.int32, (run, LANES), 1) < ATTN_HEAD_DIM
    scale = ATTN_HEAD_DIM ** -0.5 * LOG2_E
    zero = jnp.zeros((run, LANES), F32)
    one = jnp.ones((run, LANES), F32)
    pairs = ATTN_WIDTH // LANES
    for s in range(QKV_WIDTH // LANES):
        kind, p = divmod(s, pairs)
        cols = lambda group: slice((group * pairs + p) * LANES, (group * pairs + p + 1) * LANES)
        for r4 in range(4):
            tmp_sc[s, r4 * quarter:(r4 + 1) * quarter, :] = res_sc[s, pl.ds(r4, quarter, stride=4), :]
        for r16 in range(ATTN_RESIDUES):
            rows = tmp_sc[s, pl.ds((r16 % 4) * quarter + r16 // 4, run, stride=4), :]
            if kind == 0:
                q = rows * scale
                att_ref[r16, :, cols(0)] = jnp.where(head0, q, zero).astype(BF16)
                att_ref[r16, :, cols(1)] = jnp.where(head0, zero, q).astype(BF16)
            elif kind == 1:
                att_ref[r16, :, cols(2)] = rows.astype(BF16)
            else:
                att_ref[r16, :, cols(3)] = jnp.where(head0, rows, one).astype(BF16)
                att_ref[r16, :, cols(4)] = jnp.where(head0, one, rows).astype(BF16)


def _inproj(x2, g, w, tm, seq):
    T = x2.shape[0]
    tps = seq // tm
    run = tm // ATTN_RESIDUES
    return pl.pallas_call(
        _inproj_kernel,
        grid=(T // tm,),
        in_specs=[
            pl.BlockSpec((tm, D_MODEL), lambda i: (i, 0)),
            pl.BlockSpec((1, D_MODEL), lambda i: (0, 0)),
            pl.BlockSpec((D_MODEL, QKV_WIDTH + HG_WIDTH), lambda i: (0, 0),
                         pipeline_mode=pl.Buffered(1)),
        ],
        out_specs=[
            pl.BlockSpec((None, ATTN_RESIDUES, run, ATT_OPS_WIDTH),
                         lambda i: (i // tps, 0, i % tps, 0)),
            pl.BlockSpec((tm, HG_WIDTH), lambda i: (i, 0)),
        ],
        out_shape=[
            jax.ShapeDtypeStruct((T // seq, ATTN_RESIDUES, seq // ATTN_RESIDUES,
                                  ATT_OPS_WIDTH), BF16),
            jax.ShapeDtypeStruct((T, HG_WIDTH), BF16),
        ],
        scratch_shapes=[pltpu.VMEM((QKV_WIDTH // LANES, tm, LANES), F32)] * 2,
        compiler_params=pltpu.CompilerParams(
            dimension_semantics=("arbitrary",), vmem_limit_bytes=VMEM_LIMIT),
        name="inproj",
    )(x2, g, w)


def _attn_runs(d):
    g = ATTN_RESIDUES // d
    return g, ATTN_BLOCK // g


def _attn_bias():
    blk = ATTN_BLOCK
    e = jnp.arange(2 * blk) % blk
    half = jnp.arange(2 * blk) // blk
    out = []
    for d in DILATIONS:
        g, run = _attn_runs(d)
        pos = g * (e % run) + e // run
        pos_q = pos[:, None]
        pos_k = (blk * half + pos)[None, :]
        dist = blk + pos_q - pos_k
        out.append((dist >= 0) & (dist <= blk))
        out.append(pos_k <= pos_q)
    return jnp.where(jnp.stack(out), 0.0, -jnp.inf).astype(F32)


def _attn_kernel(q0_ref, q1_ref, k_ref, v0_ref, v1_ref, bias_ref, o_ref,
                 m_sc, l_sc, acc_sc, *, seq):
    blk = ATTN_BLOCK
    n_blocks = seq // blk
    ns = seq // ATTN_RESIDUES
    head0 = lax.broadcasted_iota(jnp.int32, (blk, LANES), 1) < ATTN_HEAD_DIM

    def gather(ref, runs):
        return jnp.concatenate([ref[rs, :] for rs in runs], axis=0)

    def block_runs(d, j, which):
        nb = n_blocks // d
        g, run = _attn_runs(d)
        r, n = divmod(j, nb)
        n0 = max(n - 1, 0)
        nsel = {"q": [n], "kv": [n0, n0 + 1]}[which]
        return [pl.ds((d * jj + r) * ns + run * nn, run) for nn in nsel for jj in range(g)]

    def scores_of(group):
        out = []
        for d, j in group:
            qruns = block_runs(d, j, "q")
            qs = jnp.concatenate([gather(q0_ref, qruns),
                                  gather(q1_ref, qruns)], axis=0)
            kw = gather(k_ref, block_runs(d, j, "kv"))
            out.append(lax.dot_general(qs, kw, (((1,), (1,)), ((), ())),
                                       preferred_element_type=F32))
        return out

    def finish(group, scores):
        parts = []
        for (d, j), s in zip(group, scores):
            kruns = block_runs(d, j, "kv")
            first = 1 if j % (n_blocks // d) == 0 else 0
            s = s + bias_ref[2 * DILATIONS.index(d) + first]
            m = jnp.max(s, axis=-1, keepdims=True)
            p = jnp.exp2(s - m).astype(BF16)
            pv0 = jnp.dot(p[:blk], gather(v0_ref, kruns),
                          preferred_element_type=F32)
            pv1 = jnp.dot(p[blk:], gather(v1_ref, kruns),
                          preferred_element_type=F32)
            parts.append((pv0, pv1, m))
        for (d, j), (pv0, pv1, m) in zip(group, parts):
            bi = DILATIONS.index(d)
            qruns = block_runs(d, j, "q")
            o_b = jnp.where(head0, pv0, pv1)
            l_b = jnp.where(head0, pv1, pv0)
            m_b = jnp.where(head0, jnp.broadcast_to(m[:blk], (blk, LANES)),
                            jnp.broadcast_to(m[blk:], (blk, LANES)))
            run_len = blk // len(qruns)
            for i, rs in enumerate(qruns):
                piece = slice(i * run_len, (i + 1) * run_len)
                acc_sc[bi, rs, :] = o_b[piece]
                m_sc[bi, rs, :] = m_b[piece]
                l_sc[bi, rs, :] = l_b[piece]

    blocks = [(d, j) for d in DILATIONS for j in range(n_blocks)]
    groups = [blocks[i:i + ATTN_GROUP] for i in range(0, len(blocks), ATTN_GROUP)]
    scores = scores_of(groups[0])
    for i, group in enumerate(groups):
        nxt = scores_of(groups[i + 1]) if i + 1 < len(groups) else None
        finish(group, scores)
        scores = nxt

    def merge_body(t, carry):
        rows = pl.ds(pl.multiple_of(t * blk, blk), blk)
        ms = [m_sc[bi, rows, :] for bi in range(len(DILATIONS))]
        m_all = functools.reduce(jnp.maximum, ms)
        den = jnp.zeros((blk, LANES), F32)
        num = jnp.zeros((blk, LANES), F32)
        for bi in range(len(DILATIONS)):
            w = jnp.exp2(ms[bi] - m_all)
            den = den + w * pltpu.roll(l_sc[bi, rows, :], ATTN_HEAD_DIM, axis=1)
            num = num + w * acc_sc[bi, rows, :]
        o_ref[rows, :] = num / den
        return carry

    lax.fori_loop(0, n_blocks, merge_body, 0, unroll=16)


def _attention(att_ops, bias):
    B, S, _ = att_ops.shape
    n_pairs = ATTN_WIDTH // LANES
    blockspec = lambda group: pl.BlockSpec((None, S, LANES),
                                           lambda b, p: (b, 0, group * n_pairs + p))
    return pl.pallas_call(
        functools.partial(_attn_kernel, seq=S),
        grid=(B, n_pairs),
        in_specs=[blockspec(group) for group in range(ATT_OPS_WIDTH // ATTN_WIDTH)] + [
            pl.BlockSpec(bias.shape, lambda b, p: (0, 0, 0), pipeline_mode=pl.Buffered(1)),
        ],
        out_specs=pl.BlockSpec((None, S, LANES), lambda b, p: (b, 0, p)),
        out_shape=jax.ShapeDtypeStruct((B, S, ATTN_WIDTH), F32),
        scratch_shapes=[pltpu.VMEM((len(DILATIONS), S, LANES), F32)] * 3,
        compiler_params=pltpu.CompilerParams(
            dimension_semantics=("arbitrary", "arbitrary"),
            vmem_limit_bytes=VMEM_LIMIT),
        name="dilated_attn",
    )(*([att_ops] * (ATT_OPS_WIDTH // ATTN_WIDTH)), bias)


def _hgrn_kernel(q_ref, f_ref, i_ref, g_ref, lbl_ref, gn_ref, wout_ref, wup_ref, wdown_ref,
                 o_ref, wout_bf_ref, wup_bf_ref, wdown_bf_ref, st_sc, tri_sc, *, seq):
    wout_bf_ref[...] = wout_ref[...].astype(BF16)
    wup_bf_ref[...] = wup_ref[...].astype(BF16)
    wdown_bf_ref[...] = wdown_ref[...].astype(BF16)

    R = HGRN_ROWS
    C = HGRN_CHUNK
    logits = lbl_ref[...]
    e = jnp.exp(logits - jnp.max(logits, axis=0, keepdims=True))
    lb = e[0:1] / jnp.sum(e, axis=0, keepdims=True)
    f_mid = 0.5 * (1.0 + lb)
    f_half = 0.5 * (1.0 - lb)
    gn = gn_ref[...]

    st_sc[...] = jnp.zeros_like(st_sc)
    row = lax.broadcasted_iota(jnp.int32, (R, R), 0)
    col = lax.broadcasted_iota(jnp.int32, (R, R), 1)
    causal = (row >= col) & ((row // C) == (col // C))
    tri_sc[...] = jnp.where(causal, 1.0, 0.0).astype(BF16)

    G = HGRN_GROUP
    NC = R // C

    def body(it, carry):
        rows = [pl.ds(pl.multiple_of((it * G + g) * R, R), R) for g in range(G)]
        ivs, keys, qfs, b2s = {}, {}, {}, {}
        q_decs, k_ends, decays, a_s = {}, {}, {}, {}
        o_intras, u_ts, o_inters = {}, {}, {}
        st = [st_sc[...]]

        def gates_and_prefix_sum(g):
            q = q_ref[rows[g], :].astype(F32)
            f = f_ref[rows[g], :].astype(F32)
            ivs[g] = i_ref[rows[g], :].astype(BF16)
            hq = 0.5 * q
            qfs[g] = hq + hq * jnp.tanh(hq)
            forget = f_mid + f_half * jnp.tanh(0.5 * f)
            keys[g] = 1.0 - forget
            lf = jnp.log2(forget)
            lf_hi = lf.astype(BF16)
            lf_lo = (lf - lf_hi.astype(F32)).astype(BF16)
            b2s[g] = jnp.dot(tri_sc[...], jnp.concatenate([lf_hi, lf_lo], axis=1),
                             preferred_element_type=F32)

        def intra_chunk_scores(g):
            b2 = b2s.pop(g)
            b = b2[:, :HGRN_DIM] + b2[:, HGRN_DIM:]
            decay_rows = [jnp.exp2(b[j * C + C - 1:j * C + C]) for j in range(NC)]
            decay_end = jnp.concatenate(
                [jnp.broadcast_to(d, (C, HGRN_DIM)) for d in decay_rows], axis=0)
            k_inv_f = keys.pop(g) * jnp.exp2(-b)
            q_decs[g] = (qfs.pop(g) * jnp.exp2(b)).astype(BF16)
            k_ends[g] = (k_inv_f * decay_end).astype(BF16)
            decays[g] = decay_rows
            a_s[g] = lax.dot_general(q_decs[g], k_inv_f.astype(BF16),
                                     (((1,), (1,)), ((), ())),
                                     preferred_element_type=F32)

        def intra_chunk_outputs(g):
            a = jnp.where(causal, a_s.pop(g), 0.0).astype(BF16)
            iv = ivs.pop(g)
            k_end = k_ends.pop(g)
            o_intras[g] = jnp.dot(a, iv, preferred_element_type=F32)
            u_ts[g] = [lax.dot_general(iv[j * C:(j + 1) * C], k_end[j * C:(j + 1) * C],
                                       (((0,), (0,)), ((), ())),
                                       preferred_element_type=F32)
                       for j in range(NC)]

        def state_scan(g):
            parts = []
            q_dec = q_decs.pop(g)
            for j in range(NC):
                parts.append(lax.dot_general(q_dec[j * C:(j + 1) * C], st[0].astype(BF16),
                                             (((1,), (1,)), ((), ())),
                                             preferred_element_type=F32))
                st[0] = st[0] * decays[g][j] + u_ts[g][j]
            o_inters[g] = jnp.concatenate(parts, axis=0)

        def emit(g):
            o = _rms(o_intras.pop(g) + o_inters.pop(g), gn)
            hg = 0.5 * g_ref[rows[g], :].astype(F32)
            o_ref[rows[g], :] = (o * (hg + hg * jnp.tanh(hg))).astype(o_ref.dtype)

        stages = (gates_and_prefix_sum, intra_chunk_scores, intra_chunk_outputs,
                  state_scan, emit)
        for t in range(G + len(stages) - 1):
            for k in reversed(range(len(stages))):
                g = t - k
                if 0 <= g < G:
                    stages[k](g)
        st_sc[...] = st[0]
        return carry

    lax.fori_loop(0, seq // (R * G), body, 0)


def _hgrn(hg, lb_logits, gn, weights):
    B, S, _ = hg.shape
    H = HGRN_HEADS
    n_rows = lb_logits.shape[1]
    blockspec = lambda off: pl.BlockSpec((None, S, HGRN_DIM), lambda b, h: (b, 0, h + off))
    slabs = [w.reshape(B * H, w.shape[0] // (B * H), w.shape[1]) for w in weights]
    slab_specs = [pl.BlockSpec((None,) + s.shape[1:], lambda b, h: (b * H + h, 0, 0))
                  for s in slabs]
    outs = pl.pallas_call(
        functools.partial(_hgrn_kernel, seq=S),
        grid=(B, H),
        in_specs=[
            blockspec(0), blockspec(H), blockspec(2 * H), blockspec(3 * H),
            pl.BlockSpec((None, n_rows, HGRN_DIM), lambda b, h: (h, 0, 0)),
            pl.BlockSpec((None, 1, HGRN_DIM), lambda b, h: (h, 0, 0)),
        ] + slab_specs,
        out_specs=[pl.BlockSpec((None, S, HGRN_DIM), lambda b, h: (b, 0, h))] + slab_specs,
        out_shape=[jax.ShapeDtypeStruct((B, S, HGRN_WIDTH), BF16)]
        + [jax.ShapeDtypeStruct(s.shape, BF16) for s in slabs],
        scratch_shapes=[pltpu.VMEM((HGRN_DIM, HGRN_DIM), F32),
                        pltpu.VMEM((HGRN_ROWS, HGRN_ROWS), BF16)],
        compiler_params=pltpu.CompilerParams(
            dimension_semantics=("arbitrary", "arbitrary"),
            vmem_limit_bytes=VMEM_LIMIT),
        name="hgrn2",
    )(hg, hg, hg, hg, lb_logits, gn, *slabs)
    return outs[0], [o.reshape(w.shape) for o, w in zip(outs[1:], weights)]


def _ffn_kernel(x_ref, attn_ref, rec_ref, ga_ref, wout_ref, g2_ref, wup_ref,
                cw_ref, cb_ref, wdown_ref, gf_ref, o_ref,
                h_sc, u_sc, act_sc, carry_sc, nat_sc, nat2_sc, *, tiles_per_seq):
    tm = x_ref.shape[0]
    i = pl.program_id(0)

    @pl.when(i % tiles_per_seq == 0)
    def _():
        carry_sc[...] = jnp.zeros_like(carry_sc)

    an = _rms(attn_ref[...], ga_ref[...])
    run = tm // ATTN_RESIDUES
    quarter = tm // 4
    for s in range(ATTN_WIDTH // LANES):
        for r16 in range(ATTN_RESIDUES):
            nat_sc[s, pl.ds((r16 % 4) * quarter + r16 // 4, run, stride=4), :] = (
                an[r16, :, s * LANES:(s + 1) * LANES])
        for r4 in range(4):
            nat2_sc[s, pl.ds(r4, quarter, stride=4), :] = nat_sc[s, r4 * quarter:(r4 + 1) * quarter, :]
    an = jnp.concatenate([nat2_sc[s] for s in range(ATTN_WIDTH // LANES)], axis=1)
    mix = jnp.dot(an.astype(BF16), wout_ref[:ATTN_WIDTH, :], preferred_element_type=F32)
    mix = mix + jnp.dot(rec_ref[...], wout_ref[ATTN_WIDTH:, :],
                        preferred_element_type=F32)
    h = x_ref[...] + mix
    h_sc[...] = h
    u_sc[...] = _rms(h, g2_ref[...]).astype(BF16)

    sqrt_half = math.sqrt(0.5)

    def chunk(start, width):
        u = u_sc[...]
        row = lax.broadcasted_iota(jnp.int32, (tm, width), 0)
        cols = slice(start, start + width)
        vcols = slice(D_FF + start, D_FF + start + width)
        gate = jnp.dot(u, wup_ref[:, cols], preferred_element_type=F32)
        val = jnp.dot(u, wup_ref[:, vcols], preferred_element_type=F32)
        prev = carry_sc[:, cols]
        carry_sc[:, cols] = gate[tm - SUBLANES:, :]
        p1 = jnp.broadcast_to(prev[SUBLANES - 1:SUBLANES], (tm, width))
        p2 = jnp.broadcast_to(prev[SUBLANES - 2:SUBLANES - 1], (tm, width))
        g1 = jnp.where(row == 0, p1, pltpu.roll(gate, 1, axis=0))
        g2 = jnp.where(row == 0, p2, jnp.where(row == 1, p1,
                                               pltpu.roll(gate, 2, axis=0)))
        cw = cw_ref[:, cols]
        conv = cb_ref[:, cols] + cw[0:1] * g2 + cw[1:2] * g1 + cw[2:3] * gate
        act = 0.5 * conv * (1.0 + lax.erf(conv * sqrt_half)) * val
        act_sc[:, cols] = act.astype(BF16)

    start = 0
    for width in FF_CHUNKS:
        chunk(start, width)
        start += width
    down = jnp.dot(act_sc[...], wdown_ref[...], preferred_element_type=F32)
    o_ref[...] = _rms(h_sc[...] + down, gf_ref[...])


def _ffn(x2, attn, rec2, ga, wout, g2, wup, cw, cb, wdown, gf, tm, seq):
    T = x2.shape[0]
    tps = seq // tm
    const = lambda shape: pl.BlockSpec(shape, lambda i: (0,) * len(shape),
                                       pipeline_mode=pl.Buffered(1))
    return pl.pallas_call(
        functools.partial(_ffn_kernel, tiles_per_seq=tps),
        grid=(T // tm,),
        in_specs=[
            pl.BlockSpec((tm, D_MODEL), lambda i: (i, 0)),
            pl.BlockSpec((None, ATTN_RESIDUES, tm // ATTN_RESIDUES, ATTN_WIDTH),
                         lambda i: (i // tps, 0, i % tps, 0)),
            pl.BlockSpec((tm, HGRN_WIDTH), lambda i: (i, 0)),
            const((1, ATTN_WIDTH)),
            const((D_MODEL, D_MODEL)),
            const((1, D_MODEL)),
            const((D_MODEL, 2 * D_FF)),
            const((CONV_WIDTH, D_FF)),
            const((1, D_FF)),
            const((D_FF, D_MODEL)),
            const((1, D_MODEL)),
        ],
        out_specs=pl.BlockSpec((tm, D_MODEL), lambda i: (i, 0)),
        out_shape=jax.ShapeDtypeStruct((T, D_MODEL), F32),
        scratch_shapes=[
            pltpu.VMEM((tm, D_MODEL), F32),
            pltpu.VMEM((tm, D_MODEL), BF16),
            pltpu.VMEM((tm, D_FF), BF16),
            pltpu.VMEM((SUBLANES, D_FF), F32),
            pltpu.VMEM((ATTN_WIDTH // LANES, tm, LANES), F32),
            pltpu.VMEM((ATTN_WIDTH // LANES, tm, LANES), F32),
        ],
        compiler_params=pltpu.CompilerParams(
            dimension_semantics=("arbitrary",), vmem_limit_bytes=VMEM_LIMIT),
        name="outproj_convglu",
    )(x2, attn, rec2, ga, wout, g2, wup, cw, cb, wdown, gf)


def kernel(x, norm1_g, w_in, attn_norm_g, hgrn_norm_g, hgrn_lb_logits, w_out,
           norm2_g, w_up, conv_w, conv_b, w_down, final_norm_g):
    B, S, D = x.shape
    depth = w_in.shape[0]
    assert depth == 1 and D == D_MODEL
    assert S % (ATTN_BLOCK * DILATIONS[-1]) == 0 and S % (HGRN_ROWS * HGRN_GROUP) == 0
    assert S % PROJ_TILE == 0 and S % FFN_TILE == 0
    T = B * S
    layer = 0
    x2 = x.reshape(T, D)

    att_ops, hg = _inproj(x2, norm1_g[layer].reshape(1, D), w_in[layer].astype(BF16),
                          PROJ_TILE, S)

    attn = _attention(att_ops.reshape(B, S, ATT_OPS_WIDTH), _attn_bias())
    attn = attn.reshape(B, ATTN_RESIDUES, S // ATTN_RESIDUES, ATTN_WIDTH)

    lbl = hgrn_lb_logits.astype(F32).reshape(depth + 1, HGRN_HEADS, HGRN_DIM)
    lbl = lbl.transpose(1, 0, 2)
    gn = hgrn_norm_g[layer].reshape(HGRN_HEADS, 1, HGRN_DIM)
    rec, (wout, wup, wdown) = _hgrn(hg.reshape(B, S, HG_WIDTH), lbl, gn,
                                    [w_out[layer], w_up[layer], w_down[layer]])

    cw = conv_w[layer]
    cb = conv_b[layer].reshape(1, D_FF)
    out = _ffn(x2, attn, rec.reshape(T, HGRN_WIDTH),
               attn_norm_g[layer].reshape(1, ATTN_WIDTH), wout,
               norm2_g[layer].reshape(1, D), wup, cw, cb, wdown,
               final_norm_g.reshape(1, D), FFN_TILE, S)
    return out.reshape(B, S, D)
```

```python
import functools
import math

import jax
import jax.numpy as jnp
import numpy as np
from jax import lax
from jax.experimental import pallas as pl
from jax.experimental.pallas import tpu as pltpu

F32 = jnp.float32
BF16 = jnp.bfloat16

D_MODEL = 1024
ATTN_WIDTH = 512
ATTN_HEADS = 8
ATTN_HEAD_DIM = 64
ATTN_BLOCK = 128
DILATIONS = (1, 4, 16)
ATTN_RESIDUES = 16
HGRN_WIDTH = 512
HGRN_HEADS = 4
HGRN_DIM = 128
HGRN_CHUNK = 64
D_FF = 2816
CONV_WIDTH = 3
NORM_EPS = 1e-6
LOG2_E = math.log2(math.e)
QKV_WIDTH = 3 * ATTN_WIDTH
ATT_OPS_WIDTH = 5 * ATTN_WIDTH
HG_WIDTH = 4 * HGRN_WIDTH

LANES = 128
SUBLANES = 8
MXU_COLS = 256
FF_CHUNKS = (768, 768, 768, 512)
PROJ_TILE = 1024
FFN_TILE = 512
HGRN_ROWS = 256
HGRN_GROUP = 16
ATTN_GROUP = 16
VMEM_LIMIT = 56 * 1024 * 1024


def _rms(x, g):
    return x * lax.rsqrt(jnp.mean(x * x, axis=-1, keepdims=True) + NORM_EPS) * g


def _inproj_kernel(x_ref, g_ref, w_ref, att_ref, hg_ref, res_sc, tmp_sc):
    tm = x_ref.shape[0]
    run = tm // ATTN_RESIDUES
    quarter = tm // 4
    u = _rms(x_ref[...], g_ref[...]).astype(BF16)
    n_att = QKV_WIDTH // MXU_COLS
    n_gate = HG_WIDTH // MXU_COLS
    for c in range(max(n_att, n_gate)):
        if c < n_att:
            r = jnp.dot(u, w_ref[:, c * MXU_COLS:(c + 1) * MXU_COLS].astype(BF16),
                        preferred_element_type=F32)
            res_sc[2 * c] = r[:, :LANES]
            res_sc[2 * c + 1] = r[:, LANES:]
        if c < n_gate:
            wcols = slice(QKV_WIDTH + c * MXU_COLS, QKV_WIDTH + (c + 1) * MXU_COLS)
            r = jnp.dot(u, w_ref[:, wcols].astype(BF16),
                        preferred_element_type=F32).astype(hg_ref.dtype)
            hg_ref[2 * c] = r[:, :LANES]
            hg_ref[2 * c + 1] = r[:, LANES:]
    head0 = lax.broadcasted_iota(jnp.int32, (run, LANES), 1) < ATTN_HEAD_DIM
    scale = ATTN_HEAD_DIM ** -0.5 * LOG2_E
    zero = jnp.zeros((run, LANES), F32)
    one = jnp.ones((run, LANES), F32)
    pairs = ATTN_WIDTH // LANES
    for s in range(QKV_WIDTH // LANES):
        kind, p = divmod(s, pairs)
        cols = lambda group: slice((group * pairs + p) * LANES, (group * pairs + p + 1) * LANES)
        for r4 in range(4):
            tmp_sc[s, r4 * quarter:(r4 + 1) * quarter, :] = res_sc[s, pl.ds(r4, quarter, stride=4), :]
        for r16 in range(ATTN_RESIDUES):
            rows = tmp_sc[s, pl.ds((r16 % 4) * quarter + r16 // 4, run, stride=4), :]
            if kind == 0:
                q = rows * scale
                att_ref[r16, :, cols(0)] = jnp.where(head0, q, zero).astype(BF16)
                att_ref[r16, :, cols(1)] = jnp.where(head0, zero, q).astype(BF16)
            elif kind == 1:
                att_ref[r16, :, cols(2)] = rows.astype(BF16)
            else:
                att_ref[r16, :, cols(3)] = jnp.where(head0, rows, one).astype(BF16)
                att_ref[r16, :, cols(4)] = jnp.where(head0, one, rows).astype(BF16)


def _inproj(x2, g, w, tm, seq):
    T = x2.shape[0]
    tps = seq // tm
    run = tm // ATTN_RESIDUES
    return pl.pallas_call(
        _inproj_kernel,
        grid=(T // tm,),
        in_specs=[
            pl.BlockSpec((tm, D_MODEL), lambda i: (i, 0)),
            pl.BlockSpec((1, D_MODEL), lambda i: (0, 0)),
            pl.BlockSpec((D_MODEL, QKV_WIDTH + HG_WIDTH), lambda i: (0, 0),
                         pipeline_mode=pl.Buffered(1)),
        ],
        out_specs=[
            pl.BlockSpec((None, ATTN_RESIDUES, run, ATT_OPS_WIDTH),
                         lambda i: (i // tps, 0, i % tps, 0)),
            pl.BlockSpec((HG_WIDTH // LANES, tm, LANES), lambda i: (0, i, 0)),
        ],
        out_shape=[
            jax.ShapeDtypeStruct((T // seq, ATTN_RESIDUES, seq // ATTN_RESIDUES,
                                  ATT_OPS_WIDTH), BF16),
            jax.ShapeDtypeStruct((HG_WIDTH // LANES, T, LANES), BF16),
        ],
        scratch_shapes=[pltpu.VMEM((QKV_WIDTH // LANES, tm, LANES), F32),
                        pltpu.VMEM((QKV_WIDTH // LANES, tm + SUBLANES, LANES), F32)],
        compiler_params=pltpu.CompilerParams(
            dimension_semantics=("arbitrary",), vmem_limit_bytes=VMEM_LIMIT),
        name="inproj",
    )(x2, g, w)


def _attn_runs(d):
    g = ATTN_RESIDUES // d
    return g, ATTN_BLOCK // g


def _attn_bias():
    blk = ATTN_BLOCK
    e = np.arange(2 * blk) % blk
    half = np.arange(2 * blk) // blk
    out = []
    for d in DILATIONS:
        g, run = _attn_runs(d)
        pos = g * (e % run) + e // run
        pos_q = pos[:, None]
        pos_k = (blk * half + pos)[None, :]
        dist = blk + pos_q - pos_k
        out.append((dist >= 0) & (dist <= blk))
        out.append(pos_k <= pos_q)
    return np.where(np.stack(out), 0.0, -np.inf).astype(np.float32)


def _attn_kernel(q0_ref, q1_ref, k_ref, v0_ref, v1_ref, bias_ref, o_ref,
                 m_sc, l_sc, acc_sc, *, seq):
    blk = ATTN_BLOCK
    n_blocks = seq // blk
    ns = seq // ATTN_RESIDUES
    head0 = lax.broadcasted_iota(jnp.int32, (blk, LANES), 1) < ATTN_HEAD_DIM

    def gather(ref, runs):
        return jnp.concatenate([ref[rs, :] for rs in runs], axis=0)

    def block_runs(d, j, which):
        nb = n_blocks // d
        g, run = _attn_runs(d)
        r, n = divmod(j, nb)
        n0 = max(n - 1, 0)
        nsel = {"q": [n], "kv": [n0, n0 + 1]}[which]
        return [pl.ds((d * jj + r) * ns + run * nn, run) for nn in nsel for jj in range(g)]

    def scores_of(group):
        out = []
        for d, j in group:
            qruns = block_runs(d, j, "q")
            qs = jnp.concatenate([gather(q0_ref, qruns),
                                  gather(q1_ref, qruns)], axis=0)
            kw = gather(k_ref, block_runs(d, j, "kv"))
            out.append(lax.dot_general(qs, kw, (((1,), (1,)), ((), ())),
                                       preferred_element_type=F32))
        return out

    def finish(group, scores):
        parts = []
        for (d, j), s in zip(group, scores):
            kruns = block_runs(d, j, "kv")
            first = 1 if j % (n_blocks // d) == 0 else 0
            s = s + bias_ref[2 * DILATIONS.index(d) + first]
            m = jnp.max(s, axis=-1, keepdims=True)
            p = jnp.exp2(s - m).astype(BF16)
            pv0 = jnp.dot(p[:blk], gather(v0_ref, kruns),
                          preferred_element_type=F32)
            pv1 = jnp.dot(p[blk:], gather(v1_ref, kruns),
                          preferred_element_type=F32)
            parts.append((pv0, pv1, m))
        for (d, j), (pv0, pv1, m) in zip(group, parts):
            bi = DILATIONS.index(d)
            qruns = block_runs(d, j, "q")
            o_b = jnp.where(head0, pv0, pv1)
            l_b = jnp.where(head0, pv1, pv0)
            m_b = jnp.where(head0, jnp.broadcast_to(m[:blk], (blk, LANES)),
                            jnp.broadcast_to(m[blk:], (blk, LANES)))
            run_len = blk // len(qruns)
            for i, rs in enumerate(qruns):
                piece = slice(i * run_len, (i + 1) * run_len)
                acc_sc[bi, rs, :] = o_b[piece]
                m_sc[bi, rs, :] = m_b[piece]
                l_sc[bi, rs, :] = l_b[piece]

    blocks = [(d, j) for d in DILATIONS for j in range(n_blocks)]
    groups = [blocks[i:i + ATTN_GROUP] for i in range(0, len(blocks), ATTN_GROUP)]
    scores = scores_of(groups[0])
    for i, group in enumerate(groups):
        nxt = scores_of(groups[i + 1]) if i + 1 < len(groups) else None
        finish(group, scores)
        scores = nxt

    def merge_body(t, carry):
        rows = pl.ds(pl.multiple_of(t * blk, blk), blk)
        ms = [m_sc[bi, rows, :] for bi in range(len(DILATIONS))]
        m_all = functools.reduce(jnp.maximum, ms)
        den = jnp.zeros((blk, LANES), F32)
        num = jnp.zeros((blk, LANES), F32)
        for bi in range(len(DILATIONS)):
            w = jnp.exp2(ms[bi] - m_all)
            den = den + w * pltpu.roll(l_sc[bi, rows, :], ATTN_HEAD_DIM, axis=1)
            num = num + w * acc_sc[bi, rows, :]
        o_ref[rows, :] = num / den
        return carry

    lax.fori_loop(0, n_blocks, merge_body, 0, unroll=16)


def _attention(att_ops, bias):
    B, S, _ = att_ops.shape
    n_pairs = ATTN_WIDTH // LANES
    blockspec = lambda group: pl.BlockSpec((None, S, LANES),
                                           lambda b, p: (b, 0, group * n_pairs + p))
    return pl.pallas_call(
        functools.partial(_attn_kernel, seq=S),
        grid=(B, n_pairs),
        in_specs=[blockspec(group) for group in range(ATT_OPS_WIDTH // ATTN_WIDTH)] + [
            pl.BlockSpec(bias.shape, lambda b, p: (0, 0, 0), pipeline_mode=pl.Buffered(1)),
        ],
        out_specs=pl.BlockSpec((None, S, LANES), lambda b, p: (b, 0, p)),
        out_shape=jax.ShapeDtypeStruct((B, S, ATTN_WIDTH), F32),
        scratch_shapes=[pltpu.VMEM((len(DILATIONS), S, LANES), F32)] * 3,
        compiler_params=pltpu.CompilerParams(
            dimension_semantics=("arbitrary", "arbitrary"),
            vmem_limit_bytes=VMEM_LIMIT),
        name="dilated_attn",
    )(*([att_ops] * (ATT_OPS_WIDTH // ATTN_WIDTH)), bias)


def _hgrn_kernel(q_ref, f_ref, i_ref, g_ref, lbl_ref, gn_ref, wout_ref, wup_ref, wdown_ref,
                 o_ref, wout_bf_ref, wup_bf_ref, wdown_bf_ref, st_sc, tri_sc, *, seq):
    wout_bf_ref[...] = wout_ref[...].astype(BF16)
    wup_bf_ref[...] = wup_ref[...].astype(BF16)
    wdown_bf_ref[...] = wdown_ref[...].astype(BF16)

    R = HGRN_ROWS
    C = HGRN_CHUNK
    logits = lbl_ref[...]
    e = jnp.exp(logits - jnp.max(logits, axis=0, keepdims=True))
    lb = e[0:1] / jnp.sum(e, axis=0, keepdims=True)
    f_mid = 0.5 * (1.0 + lb)
    f_half = 0.5 * (1.0 - lb)
    gn = gn_ref[...]

    st_sc[...] = jnp.zeros_like(st_sc)
    row = lax.broadcasted_iota(jnp.int32, (R, R), 0)
    col = lax.broadcasted_iota(jnp.int32, (R, R), 1)
    causal = (row >= col) & ((row // C) == (col // C))
    tri_sc[...] = jnp.where(causal, 1.0, 0.0).astype(BF16)

    G = HGRN_GROUP
    NC = R // C

    def body(it, carry):
        rows = [pl.ds(pl.multiple_of((it * G + g) * R, R), R) for g in range(G)]
        ivs, keys, qfs, b2s = {}, {}, {}, {}
        q_decs, k_ends, decays, a_s = {}, {}, {}, {}
        o_intras, u_ts, o_inters = {}, {}, {}
        st = [st_sc[...]]

        def gates_and_prefix_sum(g):
            q = q_ref[rows[g], :].astype(F32)
            f = f_ref[rows[g], :].astype(F32)
            ivs[g] = i_ref[rows[g], :].astype(BF16)
            hq = 0.5 * q
            qfs[g] = hq + hq * jnp.tanh(hq)
            forget = f_mid + f_half * jnp.tanh(0.5 * f)
            keys[g] = 1.0 - forget
            lf = jnp.log2(forget)
            lf_hi = lf.astype(BF16)
            lf_lo = (lf - lf_hi.astype(F32)).astype(BF16)
            b2s[g] = jnp.dot(tri_sc[...], jnp.concatenate([lf_hi, lf_lo], axis=1),
                             preferred_element_type=F32)

        def intra_chunk_scores(g):
            b2 = b2s.pop(g)
            b = b2[:, :HGRN_DIM] + b2[:, HGRN_DIM:]
            decay_rows = [jnp.exp2(b[j * C + C - 1:j * C + C]) for j in range(NC)]
            decay_end = jnp.concatenate(
                [jnp.broadcast_to(d, (C, HGRN_DIM)) for d in decay_rows], axis=0)
            k_inv_f = keys.pop(g) * jnp.exp2(-b)
            q_decs[g] = (qfs.pop(g) * jnp.exp2(b)).astype(BF16)
            k_ends[g] = (k_inv_f * decay_end).astype(BF16)
            decays[g] = decay_rows
            a_s[g] = lax.dot_general(q_decs[g], k_inv_f.astype(BF16),
                                     (((1,), (1,)), ((), ())),
                                     preferred_element_type=F32)

        def intra_chunk_outputs(g):
            a = jnp.where(causal, a_s.pop(g), 0.0).astype(BF16)
            iv = ivs.pop(g)
            k_end = k_ends.pop(g)
            o_intras[g] = jnp.dot(a, iv, preferred_element_type=F32)
            u_ts[g] = [lax.dot_general(iv[j * C:(j + 1) * C], k_end[j * C:(j + 1) * C],
                                       (((0,), (0,)), ((), ())),
                                       preferred_element_type=F32)
                       for j in range(NC)]

        def state_scan(g):
            parts = []
            q_dec = q_decs.pop(g)
            for j in range(NC):
                parts.append(lax.dot_general(q_dec[j * C:(j + 1) * C], st[0].astype(BF16),
                                             (((1,), (1,)), ((), ())),
                                             preferred_element_type=F32))
                st[0] = st[0] * decays[g][j] + u_ts[g][j]
            o_inters[g] = jnp.concatenate(parts, axis=0)

        def emit(g):
            o = _rms(o_intras.pop(g) + o_inters.pop(g), gn)
            hg = 0.5 * g_ref[rows[g], :].astype(F32)
            o_ref[rows[g], :] = (o * (hg + hg * jnp.tanh(hg))).astype(o_ref.dtype)

        stages = (gates_and_prefix_sum, intra_chunk_scores, intra_chunk_outputs,
                  state_scan, emit)
        for t in range(G + len(stages) - 1):
            for k in reversed(range(len(stages))):
                g = t - k
                if 0 <= g < G:
                    stages[k](g)
        st_sc[...] = st[0]
        return carry

    lax.fori_loop(0, seq // (R * G), body, 0)


def _hgrn(hg, lb_logits, gn, weights):
    _, B, S, _ = hg.shape
    H = HGRN_HEADS
    n_rows = lb_logits.shape[1]
    blockspec = lambda off: pl.BlockSpec((None, None, S, HGRN_DIM),
                                         lambda b, h: (h + off, b, 0, 0))
    slabs = [w.reshape(B * H, w.shape[0] // (B * H), w.shape[1]) for w in weights]
    slab_specs = [pl.BlockSpec((None,) + s.shape[1:], lambda b, h: (b * H + h, 0, 0))
                  for s in slabs]
    outs = pl.pallas_call(
        functools.partial(_hgrn_kernel, seq=S),
        grid=(B, H),
        in_specs=[
            blockspec(0), blockspec(H), blockspec(2 * H), blockspec(3 * H),
            pl.BlockSpec((None, n_rows, HGRN_DIM), lambda b, h: (h, 0, 0)),
            pl.BlockSpec((None, 1, HGRN_DIM), lambda b, h: (h, 0, 0)),
        ] + slab_specs,
        out_specs=[pl.BlockSpec((None, None, S, HGRN_DIM), lambda b, h: (h, b, 0, 0))] + slab_specs,
        out_shape=[jax.ShapeDtypeStruct((H, B, S, HGRN_DIM), BF16)]
        + [jax.ShapeDtypeStruct(s.shape, BF16) for s in slabs],
        scratch_shapes=[pltpu.VMEM((HGRN_DIM, HGRN_DIM), F32),
                        pltpu.VMEM((HGRN_ROWS, HGRN_ROWS), BF16)],
        compiler_params=pltpu.CompilerParams(
            dimension_semantics=("arbitrary", "arbitrary"),
            vmem_limit_bytes=VMEM_LIMIT),
        name="hgrn2",
    )(hg, hg, hg, hg, lb_logits, gn, *slabs)
    return outs[0], [o.reshape(w.shape) for o, w in zip(outs[1:], weights)]


def _ffn_kernel(x_ref, attn_ref, rec_ref, ga_ref, wout_ref, g2_ref, wup_ref,
                cw_ref, cb_ref, wdown_ref, gf_ref, o_ref,
                h_sc, u_sc, act_sc, carry_sc, nat_sc, nat2_sc, *, tiles_per_seq):
    tm = x_ref.shape[0]
    i = pl.program_id(0)

    @pl.when(i % tiles_per_seq == 0)
    def _():
        carry_sc[...] = jnp.zeros_like(carry_sc)

    an = _rms(attn_ref[...], ga_ref[...])
    run = tm // ATTN_RESIDUES
    quarter = tm // 4
    for s in range(ATTN_WIDTH // LANES):
        for r16 in range(ATTN_RESIDUES):
            nat_sc[s, pl.ds((r16 % 4) * quarter + r16 // 4, run, stride=4), :] = (
                an[r16, :, s * LANES:(s + 1) * LANES])
        for r4 in range(4):
            nat2_sc[s, pl.ds(r4, quarter, stride=4), :] = nat_sc[s, r4 * quarter:(r4 + 1) * quarter, :]
    an = jnp.concatenate([nat2_sc[s] for s in range(ATTN_WIDTH // LANES)], axis=1)
    mix = jnp.dot(an.astype(BF16), wout_ref[:ATTN_WIDTH, :], preferred_element_type=F32)
    rec = jnp.concatenate([rec_ref[h] for h in range(HGRN_HEADS)], axis=1)
    mix = mix + jnp.dot(rec, wout_ref[ATTN_WIDTH:, :],
                        preferred_element_type=F32)
    h = x_ref[...] + mix
    h_sc[...] = h
    u_sc[...] = _rms(h, g2_ref[...]).astype(BF16)

    sqrt_half = math.sqrt(0.5)

    def chunk(start, width):
        u = u_sc[...]
        row = lax.broadcasted_iota(jnp.int32, (tm, width), 0)
        cols = slice(start, start + width)
        vcols = slice(D_FF + start, D_FF + start + width)
        gate = jnp.dot(u, wup_ref[:, cols], preferred_element_type=F32)
        val = jnp.dot(u, wup_ref[:, vcols], preferred_element_type=F32)
        prev = carry_sc[:, cols]
        carry_sc[:, cols] = gate[tm - SUBLANES:, :]
        p1 = jnp.broadcast_to(prev[SUBLANES - 1:SUBLANES], (tm, width))
        p2 = jnp.broadcast_to(prev[SUBLANES - 2:SUBLANES - 1], (tm, width))
        g1 = jnp.where(row == 0, p1, pltpu.roll(gate, 1, axis=0))
        g2 = jnp.where(row == 0, p2, jnp.where(row == 1, p1,
                                               pltpu.roll(gate, 2, axis=0)))
        cw = cw_ref[:, cols]
        conv = cb_ref[:, cols] + cw[0:1] * g2 + cw[1:2] * g1 + cw[2:3] * gate
        act = 0.5 * conv * (1.0 + lax.erf(conv * sqrt_half)) * val
        act_sc[:, cols] = act.astype(BF16)

    start = 0
    for width in FF_CHUNKS:
        chunk(start, width)
        start += width
    down = jnp.dot(act_sc[...], wdown_ref[...], preferred_element_type=F32)
    o_ref[...] = _rms(h_sc[...] + down, gf_ref[...])


def _ffn(x2, attn, rec2, ga, wout, g2, wup, cw, cb, wdown, gf, tm, seq):
    T = x2.shape[0]
    tps = seq // tm
    const = lambda shape: pl.BlockSpec(shape, lambda i: (0,) * len(shape),
                                       pipeline_mode=pl.Buffered(1))
    return pl.pallas_call(
        functools.partial(_ffn_kernel, tiles_per_seq=tps),
        grid=(T // tm,),
        in_specs=[
            pl.BlockSpec((tm, D_MODEL), lambda i: (i, 0)),
            pl.BlockSpec((None, ATTN_RESIDUES, tm // ATTN_RESIDUES, ATTN_WIDTH),
                         lambda i: (i // tps, 0, i % tps, 0)),
            pl.BlockSpec((HGRN_HEADS, tm, HGRN_DIM), lambda i: (0, i, 0)),
            const((1, ATTN_WIDTH)),
            const((D_MODEL, D_MODEL)),
            const((1, D_MODEL)),
            const((D_MODEL, 2 * D_FF)),
            const((CONV_WIDTH, D_FF)),
            const((1, D_FF)),
            const((D_FF, D_MODEL)),
            const((1, D_MODEL)),
        ],
        out_specs=pl.BlockSpec((tm, D_MODEL), lambda i: (i, 0)),
        out_shape=jax.ShapeDtypeStruct((T, D_MODEL), F32),
        scratch_shapes=[
            pltpu.VMEM((tm, D_MODEL), F32),
            pltpu.VMEM((tm, D_MODEL), BF16),
            pltpu.VMEM((tm, D_FF), BF16),
            pltpu.VMEM((SUBLANES, D_FF), F32),
            pltpu.VMEM((ATTN_WIDTH // LANES, tm, LANES), F32),
            pltpu.VMEM((ATTN_WIDTH // LANES, tm, LANES), F32),
        ],
        compiler_params=pltpu.CompilerParams(
            dimension_semantics=("arbitrary",), vmem_limit_bytes=VMEM_LIMIT),
        name="outproj_convglu",
    )(x2, attn, rec2, ga, wout, g2, wup, cw, cb, wdown, gf)


def kernel(x, norm1_g, w_in, attn_norm_g, hgrn_norm_g, hgrn_lb_logits, w_out,
           norm2_g, w_up, conv_w, conv_b, w_down, final_norm_g):
    B, S, D = x.shape
    depth = w_in.shape[0]
    assert depth == 1 and D == D_MODEL
    assert S % (ATTN_BLOCK * DILATIONS[-1]) == 0 and S % (HGRN_ROWS * HGRN_GROUP) == 0
    assert S % PROJ_TILE == 0 and S % FFN_TILE == 0
    T = B * S
    layer = 0
    x2 = x.reshape(T, D)

    att_ops, hg = _inproj(x2, norm1_g[layer].reshape(1, D), w_in[layer],
                          PROJ_TILE, S)

    attn = _attention(att_ops.reshape(B, S, ATT_OPS_WIDTH), _attn_bias())
    attn = attn.reshape(B, ATTN_RESIDUES, S // ATTN_RESIDUES, ATTN_WIDTH)

    lbl = hgrn_lb_logits.astype(F32).reshape(depth + 1, HGRN_HEADS, HGRN_DIM)
    lbl = lbl.transpose(1, 0, 2)
    gn = hgrn_norm_g[layer].reshape(HGRN_HEADS, 1, HGRN_DIM)
    rec, (wout, wup, wdown) = _hgrn(hg.reshape(HG_WIDTH // LANES, B, S, LANES), lbl, gn,
                                    [w_out[layer], w_up[layer], w_down[layer]])

    cw = conv_w[layer]
    cb = conv_b[layer].reshape(1, D_FF)
    out = _ffn(x2, attn, rec.reshape(HGRN_HEADS, T, HGRN_DIM),
               attn_norm_g[layer].reshape(1, ATTN_WIDTH), wout,
               norm2_g[layer].reshape(1, D), wup, cw, cb, wdown,
               final_norm_g.reshape(1, D), FFN_TILE, S)
    return out.reshape(B, S, D)
```

```python
import functools
import math

import jax
import jax.numpy as jnp
import numpy as np
from jax import lax
from jax.experimental import pallas as pl
from jax.experimental.pallas import tpu as pltpu

F32 = jnp.float32
BF16 = jnp.bfloat16

D_MODEL = 1024
ATTN_WIDTH = 512
ATTN_HEADS = 8
ATTN_HEAD_DIM = 64
ATTN_BLOCK = 128
DILATIONS = (1, 4, 16)
ATTN_RESIDUES = 16
HGRN_WIDTH = 512
HGRN_HEADS = 4
HGRN_DIM = 128
HGRN_CHUNK = 64
D_FF = 2816
CONV_WIDTH = 3
NORM_EPS = 1e-6
LOG2_E = math.log2(math.e)
QKV_WIDTH = 3 * ATTN_WIDTH
ATT_OPS_WIDTH = 5 * ATTN_WIDTH
HG_WIDTH = 4 * HGRN_WIDTH

LANES = 128
SUBLANES = 8
MXU_COLS = 256
FF_CHUNKS = (768, 768, 768, 512)
PROJ_TILE = 1024
FFN_TILE = 512
HGRN_ROWS = 256
HGRN_HEADS_PER_STEP = 2
ATTN_GROUP = 16
VMEM_LIMIT = 56 * 1024 * 1024


def _rms(x, g):
    return x * lax.rsqrt(jnp.mean(x * x, axis=-1, keepdims=True) + NORM_EPS) * g


def _inproj_kernel(x_ref, g_ref, w_ref, att_ref, hg_ref, res_sc, tmp_sc):
    tm = x_ref.shape[0]
    run = tm // ATTN_RESIDUES
    quarter = tm // 4
    u = _rms(x_ref[...], g_ref[...]).astype(BF16)
    n_att = QKV_WIDTH // MXU_COLS
    n_gate = HG_WIDTH // MXU_COLS
    for c in range(max(n_att, n_gate)):
        if c < n_att:
            r = jnp.dot(u, w_ref[:, c * MXU_COLS:(c + 1) * MXU_COLS].astype(BF16),
                        preferred_element_type=F32)
            res_sc[2 * c] = r[:, :LANES]
            res_sc[2 * c + 1] = r[:, LANES:]
        if c < n_gate:
            wcols = slice(QKV_WIDTH + c * MXU_COLS, QKV_WIDTH + (c + 1) * MXU_COLS)
            r = jnp.dot(u, w_ref[:, wcols].astype(BF16),
                        preferred_element_type=F32).astype(hg_ref.dtype)
            hg_ref[2 * c] = r[:, :LANES]
            hg_ref[2 * c + 1] = r[:, LANES:]
    head0 = lax.broadcasted_iota(jnp.int32, (run, LANES), 1) < ATTN_HEAD_DIM
    scale = ATTN_HEAD_DIM ** -0.5 * LOG2_E
    zero = jnp.zeros((run, LANES), F32)
    one = jnp.ones((run, LANES), F32)
    pairs = ATTN_WIDTH // LANES
    for s in range(QKV_WIDTH // LANES):
        kind, p = divmod(s, pairs)
        cols = lambda group: slice((group * pairs + p) * LANES, (group * pairs + p + 1) * LANES)
        for r4 in range(4):
            tmp_sc[s, r4 * quarter:(r4 + 1) * quarter, :] = res_sc[s, pl.ds(r4, quarter, stride=4), :]
        for r16 in range(ATTN_RESIDUES):
            rows = tmp_sc[s, pl.ds((r16 % 4) * quarter + r16 // 4, run, stride=4), :]
            if kind == 0:
                q = rows * scale
                att_ref[r16, :, cols(0)] = jnp.where(head0, q, zero).astype(BF16)
                att_ref[r16, :, cols(1)] = jnp.where(head0, zero, q).astype(BF16)
            elif kind == 1:
                att_ref[r16, :, cols(2)] = rows.astype(BF16)
            else:
                att_ref[r16, :, cols(3)] = jnp.where(head0, rows, one).astype(BF16)
                att_ref[r16, :, cols(4)] = jnp.where(head0, one, rows).astype(BF16)


def _inproj(x2, g, w, tm, seq):
    T = x2.shape[0]
    tps = seq // tm
    run = tm // ATTN_RESIDUES
    return pl.pallas_call(
        _inproj_kernel,
        grid=(T // tm,),
        in_specs=[
            pl.BlockSpec((tm, D_MODEL), lambda i: (i, 0)),
            pl.BlockSpec((1, D_MODEL), lambda i: (0, 0)),
            pl.BlockSpec((D_MODEL, QKV_WIDTH + HG_WIDTH), lambda i: (0, 0),
                         pipeline_mode=pl.Buffered(1)),
        ],
        out_specs=[
            pl.BlockSpec((None, ATTN_RESIDUES, run, ATT_OPS_WIDTH),
                         lambda i: (i // tps, 0, i % tps, 0)),
            pl.BlockSpec((HG_WIDTH // LANES, tm, LANES), lambda i: (0, i, 0)),
        ],
        out_shape=[
            jax.ShapeDtypeStruct((T // seq, ATTN_RESIDUES, seq // ATTN_RESIDUES,
                                  ATT_OPS_WIDTH), BF16),
            jax.ShapeDtypeStruct((HG_WIDTH // LANES, T, LANES), BF16),
        ],
        scratch_shapes=[pltpu.VMEM((QKV_WIDTH // LANES, tm, LANES), F32),
                        pltpu.VMEM((QKV_WIDTH // LANES, tm + SUBLANES, LANES), F32)],
        compiler_params=pltpu.CompilerParams(
            dimension_semantics=("arbitrary",), vmem_limit_bytes=VMEM_LIMIT),
        name="inproj",
    )(x2, g, w)


def _attn_runs(d):
    g = ATTN_RESIDUES // d
    return g, ATTN_BLOCK // g


def _attn_bias():
    blk = ATTN_BLOCK
    e = np.arange(2 * blk) % blk
    half = np.arange(2 * blk) // blk
    out = []
    for d in DILATIONS:
        g, run = _attn_runs(d)
        pos = g * (e % run) + e // run
        pos_q = pos[:, None]
        pos_k = (blk * half + pos)[None, :]
        dist = blk + pos_q - pos_k
        out.append((dist >= 0) & (dist <= blk))
        out.append(pos_k <= pos_q)
    return np.where(np.stack(out), 0.0, -np.inf).astype(np.float32)


def _attn_kernel(q0_ref, q1_ref, k_ref, v0_ref, v1_ref, bias_ref, o_ref,
                 m_sc, l_sc, acc_sc, *, seq):
    blk = ATTN_BLOCK
    n_blocks = seq // blk
    ns = seq // ATTN_RESIDUES
    head0 = lax.broadcasted_iota(jnp.int32, (blk, LANES), 1) < ATTN_HEAD_DIM

    def gather(ref, runs):
        return jnp.concatenate([ref[rs, :] for rs in runs], axis=0)

    def block_runs(d, j, which):
        nb = n_blocks // d
        g, run = _attn_runs(d)
        r, n = divmod(j, nb)
        n0 = max(n - 1, 0)
        nsel = {"q": [n], "kv": [n0, n0 + 1]}[which]
        return [pl.ds((d * jj + r) * ns + run * nn, run) for nn in nsel for jj in range(g)]

    def scores_of(group):
        out = []
        for d, j in group:
            qruns = block_runs(d, j, "q")
            qs = jnp.concatenate([gather(q0_ref, qruns),
                                  gather(q1_ref, qruns)], axis=0)
            kw = gather(k_ref, block_runs(d, j, "kv"))
            out.append(lax.dot_general(qs, kw, (((1,), (1,)), ((), ())),
                                       preferred_element_type=F32))
        return out

    def finish(group, scores):
        parts = []
        for (d, j), s in zip(group, scores):
            kruns = block_runs(d, j, "kv")
            first = 1 if j % (n_blocks // d) == 0 else 0
            s = s + bias_ref[2 * DILATIONS.index(d) + first]
            m = jnp.max(s, axis=-1, keepdims=True)
            p = jnp.exp2(s - m).astype(BF16)
            pv0 = jnp.dot(p[:blk], gather(v0_ref, kruns),
                          preferred_element_type=F32)
            pv1 = jnp.dot(p[blk:], gather(v1_ref, kruns),
                          preferred_element_type=F32)
            parts.append((pv0, pv1, m))
        for (d, j), (pv0, pv1, m) in zip(group, parts):
            bi = DILATIONS.index(d)
            qruns = block_runs(d, j, "q")
            o_b = jnp.where(head0, pv0, pv1)
            l_b = jnp.where(head0, pv1, pv0)
            m_b = jnp.where(head0, jnp.broadcast_to(m[:blk], (blk, LANES)),
                            jnp.broadcast_to(m[blk:], (blk, LANES)))
            run_len = blk // len(qruns)
            for i, rs in enumerate(qruns):
                piece = slice(i * run_len, (i + 1) * run_len)
                acc_sc[bi, rs, :] = o_b[piece]
                m_sc[bi, rs, :] = m_b[piece]
                l_sc[bi, rs, :] = l_b[piece]

    blocks = [(d, j) for d in DILATIONS for j in range(n_blocks)]
    groups = [blocks[i:i + ATTN_GROUP] for i in range(0, len(blocks), ATTN_GROUP)]
    scores = scores_of(groups[0])
    for i, group in enumerate(groups):
        nxt = scores_of(groups[i + 1]) if i + 1 < len(groups) else None
        finish(group, scores)
        scores = nxt

    def merge_body(t, carry):
        rows = pl.ds(pl.multiple_of(t * blk, blk), blk)
        ms = [m_sc[bi, rows, :] for bi in range(len(DILATIONS))]
        m_all = functools.reduce(jnp.maximum, ms)
        den = jnp.zeros((blk, LANES), F32)
        num = jnp.zeros((blk, LANES), F32)
        for bi in range(len(DILATIONS)):
            w = jnp.exp2(ms[bi] - m_all)
            den = den + w * pltpu.roll(l_sc[bi, rows, :], ATTN_HEAD_DIM, axis=1)
            num = num + w * acc_sc[bi, rows, :]
        o_ref[rows, :] = num / den
        return carry

    lax.fori_loop(0, n_blocks, merge_body, 0, unroll=16)


def _attention(att_ops, bias):
    B, S, _ = att_ops.shape
    n_pairs = ATTN_WIDTH // LANES
    blockspec = lambda group: pl.BlockSpec((None, S, LANES),
                                           lambda b, p: (b, 0, group * n_pairs + p))
    return pl.pallas_call(
        functools.partial(_attn_kernel, seq=S),
        grid=(B, n_pairs),
        in_specs=[blockspec(group) for group in range(ATT_OPS_WIDTH // ATTN_WIDTH)] + [
            pl.BlockSpec(bias.shape, lambda b, p: (0, 0, 0), pipeline_mode=pl.Buffered(1)),
        ],
        out_specs=pl.BlockSpec((None, S, LANES), lambda b, p: (b, 0, p)),
        out_shape=jax.ShapeDtypeStruct((B, S, ATTN_WIDTH), F32),
        scratch_shapes=[pltpu.VMEM((len(DILATIONS), S, LANES), F32)] * 3,
        compiler_params=pltpu.CompilerParams(
            dimension_semantics=("arbitrary", "arbitrary"),
            vmem_limit_bytes=VMEM_LIMIT),
        name="dilated_attn",
    )(*([att_ops] * (ATT_OPS_WIDTH // ATTN_WIDTH)), bias)


def _hgrn_kernel(q_ref, f_ref, i_ref, g_ref, lbl_ref, gn_ref, wout_ref, wup_ref, wdown_ref,
                 o_ref, wout_bf_ref, wup_bf_ref, wdown_bf_ref, tri_sc, *, seq):
    wout_bf_ref[...] = wout_ref[...].astype(BF16)
    wup_bf_ref[...] = wup_ref[...].astype(BF16)
    wdown_bf_ref[...] = wdown_ref[...].astype(BF16)

    R = HGRN_ROWS
    C = HGRN_CHUNK
    NC = R // C
    G = seq // R
    HS = HGRN_HEADS_PER_STEP
    rows_of = lambda g: slice(g * R, (g + 1) * R)
    f_mid, f_half = [], []
    for hh in range(HS):
        logits = lbl_ref[hh]
        e = jnp.exp(logits - jnp.max(logits, axis=0, keepdims=True))
        lb = e[0:1] / jnp.sum(e, axis=0, keepdims=True)
        f_mid.append(0.5 * (1.0 + lb))
        f_half.append(0.5 * (1.0 - lb))

    row = lax.broadcasted_iota(jnp.int32, (R, R), 0)
    col = lax.broadcasted_iota(jnp.int32, (R, R), 1)
    causal = (row >= col) & ((row // C) == (col // C))
    tri_sc[...] = jnp.where(causal, 1.0, 0.0).astype(BF16)

    ivs, keys, qfs, b2s = {}, {}, {}, {}
    q_decs, k_ends, decays, a_s = {}, {}, {}, {}
    o_intras, u_ts, o_inters = {}, {}, {}
    st = [jnp.zeros((HGRN_DIM, HGRN_DIM), F32) for _ in range(HS)]

    def gates_and_prefix_sum(un):
        hh, g = un
        q = q_ref[hh, rows_of(g), :].astype(F32)
        f = f_ref[hh, rows_of(g), :].astype(F32)
        ivs[un] = i_ref[hh, rows_of(g), :].astype(BF16)
        hq = 0.5 * q
        qfs[un] = hq + hq * jnp.tanh(hq)
        forget = f_mid[hh] + f_half[hh] * jnp.tanh(0.5 * f)
        keys[un] = 1.0 - forget
        lf = jnp.log2(forget)
        lf_hi = lf.astype(BF16)
        lf_lo = (lf - lf_hi.astype(F32)).astype(BF16)
        b2s[un] = jnp.dot(tri_sc[...], jnp.concatenate([lf_hi, lf_lo], axis=1),
                          preferred_element_type=F32)

    def intra_chunk_scores(un):
        b2 = b2s.pop(un)
        b = b2[:, :HGRN_DIM] + b2[:, HGRN_DIM:]
        decay_rows = [jnp.exp2(b[j * C + C - 1:j * C + C]) for j in range(NC)]
        decay_end = jnp.concatenate(
            [jnp.broadcast_to(d, (C, HGRN_DIM)) for d in decay_rows], axis=0)
        k_inv_f = keys.pop(un) * jnp.exp2(-b)
        q_decs[un] = (qfs.pop(un) * jnp.exp2(b)).astype(BF16)
        k_ends[un] = (k_inv_f * decay_end).astype(BF16)
        decays[un] = decay_rows
        a_s[un] = lax.dot_general(q_decs[un], k_inv_f.astype(BF16),
                                  (((1,), (1,)), ((), ())),
                                  preferred_element_type=F32)

    def intra_chunk_outputs(un):
        a = jnp.where(causal, a_s.pop(un), 0.0).astype(BF16)
        iv = ivs.pop(un)
        k_end = k_ends.pop(un)
        o_intras[un] = jnp.dot(a, iv, preferred_element_type=F32)
        u_ts[un] = [lax.dot_general(iv[j * C:(j + 1) * C], k_end[j * C:(j + 1) * C],
                                    (((0,), (0,)), ((), ())),
                                    preferred_element_type=F32)
                    for j in range(NC)]

    def state_scan(un):
        hh, g = un
        parts = []
        q_dec = q_decs.pop(un)
        for j in range(NC):
            parts.append(lax.dot_general(q_dec[j * C:(j + 1) * C], st[hh].astype(BF16),
                                         (((1,), (1,)), ((), ())),
                                         preferred_element_type=F32))
            st[hh] = st[hh] * decays[un][j] + u_ts[un][j]
        o_inters[un] = jnp.concatenate(parts, axis=0)

    def emit(un):
        hh, g = un
        o = _rms(o_intras.pop(un) + o_inters.pop(un), gn_ref[hh])
        hg = 0.5 * g_ref[hh, rows_of(g), :].astype(F32)
        o_ref[hh, rows_of(g), :] = (o * (hg + hg * jnp.tanh(hg))).astype(o_ref.dtype)

    units = [(hh, g) for g in range(G) for hh in range(HS)]
    stages = (gates_and_prefix_sum, intra_chunk_scores, intra_chunk_outputs,
              state_scan, emit)
    for t in range(len(units) + len(stages) - 1):
        for k in reversed(range(len(stages))):
            u = t - k
            if 0 <= u < len(units):
                stages[k](units[u])


def _hgrn(hg, lb_logits, gn, weights):
    _, B, S, _ = hg.shape
    H = HGRN_HEADS
    HS = HGRN_HEADS_PER_STEP
    n_steps = B * (H // HS)
    n_rows = lb_logits.shape[1]
    blockspec = lambda part: pl.BlockSpec((HS, None, S, HGRN_DIM),
                                          lambda b, h: (part * (H // HS) + h, b, 0, 0))
    slabs = [w.reshape(n_steps, w.shape[0] // n_steps, w.shape[1]) for w in weights]
    slab_specs = [pl.BlockSpec((None,) + s.shape[1:], lambda b, h: (b * (H // HS) + h, 0, 0))
                  for s in slabs]
    outs = pl.pallas_call(
        functools.partial(_hgrn_kernel, seq=S),
        grid=(B, H // HS),
        in_specs=[
            blockspec(0), blockspec(1), blockspec(2), blockspec(3),
            pl.BlockSpec((HS, n_rows, HGRN_DIM), lambda b, h: (h, 0, 0)),
            pl.BlockSpec((HS, 1, HGRN_DIM), lambda b, h: (h, 0, 0)),
        ] + slab_specs,
        out_specs=[pl.BlockSpec((HS, None, S, HGRN_DIM), lambda b, h: (h, b, 0, 0))] + slab_specs,
        out_shape=[jax.ShapeDtypeStruct((H, B, S, HGRN_DIM), BF16)]
        + [jax.ShapeDtypeStruct(s.shape, BF16) for s in slabs],
        scratch_shapes=[pltpu.VMEM((HGRN_ROWS, HGRN_ROWS), BF16)],
        compiler_params=pltpu.CompilerParams(
            dimension_semantics=("arbitrary", "arbitrary"),
            vmem_limit_bytes=VMEM_LIMIT),
        name="hgrn2",
    )(hg, hg, hg, hg, lb_logits, gn, *slabs)
    return outs[0], [o.reshape(w.shape) for o, w in zip(outs[1:], weights)]


def _ffn_kernel(x_ref, attn_ref, rec_ref, ga_ref, wout_ref, g2_ref, wup_ref,
                cw_ref, cb_ref, wdown_ref, gf_ref, o_ref,
                h_sc, u_sc, act_sc, carry_sc, nat_sc, nat2_sc, *, tiles_per_seq):
    tm = x_ref.shape[0]
    i = pl.program_id(0)

    @pl.when(i % tiles_per_seq == 0)
    def _():
        carry_sc[...] = jnp.zeros_like(carry_sc)

    an = _rms(attn_ref[...], ga_ref[...])
    run = tm // ATTN_RESIDUES
    quarter = tm // 4
    for s in range(ATTN_WIDTH // LANES):
        for r16 in range(ATTN_RESIDUES):
            nat_sc[s, pl.ds((r16 % 4) * quarter + r16 // 4, run, stride=4), :] = (
                an[r16, :, s * LANES:(s + 1) * LANES])
        for r4 in range(4):
            nat2_sc[s, pl.ds(r4, quarter, stride=4), :] = nat_sc[s, r4 * quarter:(r4 + 1) * quarter, :]
    an = jnp.concatenate([nat2_sc[s] for s in range(ATTN_WIDTH // LANES)], axis=1)
    mix = jnp.dot(an.astype(BF16), wout_ref[:ATTN_WIDTH, :], preferred_element_type=F32)
    rec = jnp.concatenate([rec_ref[h] for h in range(HGRN_HEADS)], axis=1)
    mix = mix + jnp.dot(rec, wout_ref[ATTN_WIDTH:, :],
                        preferred_element_type=F32)
    h = x_ref[...] + mix
    h_sc[...] = h
    u_sc[...] = _rms(h, g2_ref[...]).astype(BF16)

    sqrt_half = math.sqrt(0.5)

    def chunk(start, width):
        u = u_sc[...]
        row = lax.broadcasted_iota(jnp.int32, (tm, width), 0)
        cols = slice(start, start + width)
        vcols = slice(D_FF + start, D_FF + start + width)
        gate = jnp.dot(u, wup_ref[:, cols], preferred_element_type=F32)
        val = jnp.dot(u, wup_ref[:, vcols], preferred_element_type=F32)
        prev = carry_sc[:, cols]
        carry_sc[:, cols] = gate[tm - SUBLANES:, :]
        p1 = jnp.broadcast_to(prev[SUBLANES - 1:SUBLANES], (tm, width))
        p2 = jnp.broadcast_to(prev[SUBLANES - 2:SUBLANES - 1], (tm, width))
        g1 = jnp.where(row == 0, p1, pltpu.roll(gate, 1, axis=0))
        g2 = jnp.where(row == 0, p2, jnp.where(row == 1, p1,
                                               pltpu.roll(gate, 2, axis=0)))
        cw = cw_ref[:, cols]
        conv = cb_ref[:, cols] + cw[0:1] * g2 + cw[1:2] * g1 + cw[2:3] * gate
        act = 0.5 * conv * (1.0 + lax.erf(conv * sqrt_half)) * val
        act_sc[:, cols] = act.astype(BF16)

    start = 0
    for width in FF_CHUNKS:
        chunk(start, width)
        start += width
    down = jnp.dot(act_sc[...], wdown_ref[...], preferred_element_type=F32)
    o_ref[...] = _rms(h_sc[...] + down, gf_ref[...])


def _ffn(x2, attn, rec2, ga, wout, g2, wup, cw, cb, wdown, gf, tm, seq):
    T = x2.shape[0]
    tps = seq // tm
    const = lambda shape: pl.BlockSpec(shape, lambda i: (0,) * len(shape),
                                       pipeline_mode=pl.Buffered(1))
    return pl.pallas_call(
        functools.partial(_ffn_kernel, tiles_per_seq=tps),
        grid=(T // tm,),
        in_specs=[
            pl.BlockSpec((tm, D_MODEL), lambda i: (i, 0)),
            pl.BlockSpec((None, ATTN_RESIDUES, tm // ATTN_RESIDUES, ATTN_WIDTH),
                         lambda i: (i // tps, 0, i % tps, 0)),
            pl.BlockSpec((HGRN_HEADS, tm, HGRN_DIM), lambda i: (0, i, 0)),
            const((1, ATTN_WIDTH)),
            const((D_MODEL, D_MODEL)),
            const((1, D_MODEL)),
            const((D_MODEL, 2 * D_FF)),
            const((CONV_WIDTH, D_FF)),
            const((1, D_FF)),
            const((D_FF, D_MODEL)),
            const((1, D_MODEL)),
        ],
        out_specs=pl.BlockSpec((tm, D_MODEL), lambda i: (i, 0)),
        out_shape=jax.ShapeDtypeStruct((T, D_MODEL), F32),
        scratch_shapes=[
            pltpu.VMEM((tm, D_MODEL), F32),
            pltpu.VMEM((tm, D_MODEL), BF16),
            pltpu.VMEM((tm, D_FF), BF16),
            pltpu.VMEM((SUBLANES, D_FF), F32),
            pltpu.VMEM((ATTN_WIDTH // LANES, tm, LANES), F32),
            pltpu.VMEM((ATTN_WIDTH // LANES, tm, LANES), F32),
        ],
        compiler_params=pltpu.CompilerParams(
            dimension_semantics=("arbitrary",), vmem_limit_bytes=VMEM_LIMIT),
        name="outproj_convglu",
    )(x2, attn, rec2, ga, wout, g2, wup, cw, cb, wdown, gf)


def kernel(x, norm1_g, w_in, attn_norm_g, hgrn_norm_g, hgrn_lb_logits, w_out,
           norm2_g, w_up, conv_w, conv_b, w_down, final_norm_g):
    B, S, D = x.shape
    depth = w_in.shape[0]
    assert depth == 1 and D == D_MODEL
    assert S % (ATTN_BLOCK * DILATIONS[-1]) == 0 and S % HGRN_ROWS == 0
    assert S % PROJ_TILE == 0 and S % FFN_TILE == 0
    T = B * S
    layer = 0
    x2 = x.reshape(T, D)

    att_ops, hg = _inproj(x2, norm1_g[layer].reshape(1, D), w_in[layer],
                          PROJ_TILE, S)

    attn = _attention(att_ops.reshape(B, S, ATT_OPS_WIDTH), _attn_bias())
    attn = attn.reshape(B, ATTN_RESIDUES, S // ATTN_RESIDUES, ATTN_WIDTH)

    lbl = hgrn_lb_logits.astype(F32).reshape(depth + 1, HGRN_HEADS, HGRN_DIM)
    lbl = lbl.transpose(1, 0, 2)
    gn = hgrn_norm_g[layer].reshape(HGRN_HEADS, 1, HGRN_DIM)
    rec, (wout, wup, wdown) = _hgrn(hg.reshape(HG_WIDTH // LANES, B, S, LANES), lbl, gn,
                                    [w_out[layer], w_up[layer], w_down[layer]])

    cw = conv_w[layer]
    cb = conv_b[layer].reshape(1, D_FF)
    out = _ffn(x2, attn, rec.reshape(HGRN_HEADS, T, HGRN_DIM),
               attn_norm_g[layer].reshape(1, ATTN_WIDTH), wout,
               norm2_g[layer].reshape(1, D), wup, cw, cb, wdown,
               final_norm_g.reshape(1, D), FFN_TILE, S)
    return out.reshape(B, S, D)
```

```python
import functools
import math

import jax
import jax.numpy as jnp
import numpy as np
from jax import lax
from jax.experimental import pallas as pl
from jax.experimental.pallas import tpu as pltpu

F32 = jnp.float32
BF16 = jnp.bfloat16

D_MODEL = 1024
ATTN_WIDTH = 512
ATTN_HEADS = 8
ATTN_HEAD_DIM = 64
ATTN_BLOCK = 128
DILATIONS = (1, 4, 16)
ATTN_RESIDUES = 16
HGRN_WIDTH = 512
HGRN_HEADS = 4
HGRN_DIM = 128
HGRN_CHUNK = 64
D_FF = 2816
CONV_WIDTH = 3
NORM_EPS = 1e-6
LOG2_E = math.log2(math.e)
QKV_WIDTH = 3 * ATTN_WIDTH
ATT_OPS_WIDTH = 5 * ATTN_WIDTH
HG_WIDTH = 4 * HGRN_WIDTH

LANES = 128
SUBLANES = 8
MXU_COLS = 256
FF_CHUNKS = (768, 768, 768, 512)
PROJ_TILE = 1024
FFN_TILE = 512
HGRN_ROWS = 256
HGRN_HEADS_PER_STEP = 2
ATTN_GROUP = 16
VMEM_LIMIT = 56 * 1024 * 1024


def _rms(x, g):
    return x * lax.rsqrt(jnp.mean(x * x, axis=-1, keepdims=True) + NORM_EPS) * g


def _inproj_kernel(x_ref, g_ref, w_ref, att_ref, hg_ref, res_sc, tmp_sc):
    tm = x_ref.shape[0]
    run = tm // ATTN_RESIDUES
    quarter = tm // 4
    u32 = _rms(x_ref[...], g_ref[...])
    u = u32.astype(BF16)
    for sl in range(D_MODEL // LANES):
        res_sc[sl] = u32[:, sl * LANES:(sl + 1) * LANES]
    u4 = jnp.concatenate(
        [jnp.concatenate([res_sc[sl, pl.ds(r4, quarter, stride=4), :] for r4 in range(4)], axis=0)
         for sl in range(D_MODEL // LANES)], axis=1).astype(BF16)
    n_att = QKV_WIDTH // MXU_COLS
    n_gate = HG_WIDTH // MXU_COLS
    for c in range(max(n_att, n_gate)):
        if c < n_gate:
            wcols = slice(QKV_WIDTH + c * MXU_COLS, QKV_WIDTH + (c + 1) * MXU_COLS)
            r = jnp.dot(u, w_ref[:, wcols].astype(BF16),
                        preferred_element_type=F32).astype(hg_ref.dtype)
            hg_ref[2 * c] = r[:, :LANES]
            hg_ref[2 * c + 1] = r[:, LANES:]
        if c < n_att:
            r = jnp.dot(u4, w_ref[:, c * MXU_COLS:(c + 1) * MXU_COLS].astype(BF16),
                        preferred_element_type=F32)
            tmp_sc[2 * c, :tm, :] = r[:, :LANES]
            tmp_sc[2 * c + 1, :tm, :] = r[:, LANES:]
    head0 = lax.broadcasted_iota(jnp.int32, (run, LANES), 1) < ATTN_HEAD_DIM
    scale = ATTN_HEAD_DIM ** -0.5 * LOG2_E
    zero = jnp.zeros((run, LANES), F32)
    one = jnp.ones((run, LANES), F32)
    pairs = ATTN_WIDTH // LANES
    for s in range(QKV_WIDTH // LANES):
        kind, p = divmod(s, pairs)
        cols = lambda group: slice((group * pairs + p) * LANES, (group * pairs + p + 1) * LANES)
        for r16 in range(ATTN_RESIDUES):
            rows = tmp_sc[s, pl.ds((r16 % 4) * quarter + r16 // 4, run, stride=4), :]
            if kind == 0:
                q = rows * scale
                att_ref[r16, :, cols(0)] = jnp.where(head0, q, zero).astype(BF16)
                att_ref[r16, :, cols(1)] = jnp.where(head0, zero, q).astype(BF16)
            elif kind == 1:
                att_ref[r16, :, cols(2)] = rows.astype(BF16)
            else:
                att_ref[r16, :, cols(3)] = jnp.where(head0, rows, one).astype(BF16)
                att_ref[r16, :, cols(4)] = jnp.where(head0, one, rows).astype(BF16)


def _inproj(x2, g, w, tm, seq):
    T = x2.shape[0]
    tps = seq // tm
    run = tm // ATTN_RESIDUES
    return pl.pallas_call(
        _inproj_kernel,
        grid=(T // tm,),
        in_specs=[
            pl.BlockSpec((tm, D_MODEL), lambda i: (i, 0)),
            pl.BlockSpec((1, D_MODEL), lambda i: (0, 0)),
            pl.BlockSpec((D_MODEL, QKV_WIDTH + HG_WIDTH), lambda i: (0, 0),
                         pipeline_mode=pl.Buffered(1)),
        ],
        out_specs=[
            pl.BlockSpec((None, ATTN_RESIDUES, run, ATT_OPS_WIDTH),
                         lambda i: (i // tps, 0, i % tps, 0)),
            pl.BlockSpec((HG_WIDTH // LANES, tm, LANES), lambda i: (0, i, 0)),
        ],
        out_shape=[
            jax.ShapeDtypeStruct((T // seq, ATTN_RESIDUES, seq // ATTN_RESIDUES,
                                  ATT_OPS_WIDTH), BF16),
            jax.ShapeDtypeStruct((HG_WIDTH // LANES, T, LANES), BF16),
        ],
        scratch_shapes=[pltpu.VMEM((D_MODEL // LANES, tm, LANES), F32),
                        pltpu.VMEM((QKV_WIDTH // LANES, tm + SUBLANES, LANES), F32)],
        compiler_params=pltpu.CompilerParams(
            dimension_semantics=("arbitrary",), vmem_limit_bytes=VMEM_LIMIT),
        name="inproj",
    )(x2, g, w)


def _attn_runs(d):
    g = ATTN_RESIDUES // d
    return g, ATTN_BLOCK // g


def _attn_bias():
    blk = ATTN_BLOCK
    e = np.arange(2 * blk) % blk
    half = np.arange(2 * blk) // blk
    out = []
    for d in DILATIONS:
        g, run = _attn_runs(d)
        pos = g * (e % run) + e // run
        pos_q = pos[:, None]
        pos_k = (blk * half + pos)[None, :]
        dist = blk + pos_q - pos_k
        out.append((dist >= 0) & (dist <= blk))
        out.append(pos_k <= pos_q)
    return np.where(np.stack(out), 0.0, -np.inf).astype(np.float32)


def _attn_kernel(q0_ref, q1_ref, k_ref, v0_ref, v1_ref, bias_ref, o_ref,
                 m_sc, l_sc, acc_sc, *, seq):
    blk = ATTN_BLOCK
    n_blocks = seq // blk
    ns = seq // ATTN_RESIDUES
    head0 = lax.broadcasted_iota(jnp.int32, (blk, LANES), 1) < ATTN_HEAD_DIM

    def gather(ref, runs):
        return jnp.concatenate([ref[rs, :] for rs in runs], axis=0)

    def block_runs(d, j, which):
        nb = n_blocks // d
        g, run = _attn_runs(d)
        r, n = divmod(j, nb)
        n0 = max(n - 1, 0)
        nsel = {"q": [n], "kv": [n0, n0 + 1]}[which]
        return [pl.ds((d * jj + r) * ns + run * nn, run) for nn in nsel for jj in range(g)]

    def scores_of(group):
        out = []
        for d, j in group:
            qruns = block_runs(d, j, "q")
            qs = jnp.concatenate([gather(q0_ref, qruns),
                                  gather(q1_ref, qruns)], axis=0)
            kw = gather(k_ref, block_runs(d, j, "kv"))
            out.append(lax.dot_general(qs, kw, (((1,), (1,)), ((), ())),
                                       preferred_element_type=F32))
        return out

    def finish(group, scores):
        parts = []
        for (d, j), s in zip(group, scores):
            kruns = block_runs(d, j, "kv")
            first = 1 if j % (n_blocks // d) == 0 else 0
            s = s + bias_ref[2 * DILATIONS.index(d) + first]
            m = jnp.max(s, axis=-1, keepdims=True)
            p = jnp.exp2(s - m).astype(BF16)
            pv0 = jnp.dot(p[:blk], gather(v0_ref, kruns),
                          preferred_element_type=F32)
            pv1 = jnp.dot(p[blk:], gather(v1_ref, kruns),
                          preferred_element_type=F32)
            parts.append((pv0, pv1, m))
        for (d, j), (pv0, pv1, m) in zip(group, parts):
            bi = DILATIONS.index(d)
            qruns = block_runs(d, j, "q")
            o_b = jnp.where(head0, pv0, pv1)
            l_b = jnp.where(head0, pv1, pv0)
            m_b = jnp.where(head0, jnp.broadcast_to(m[:blk], (blk, LANES)),
                            jnp.broadcast_to(m[blk:], (blk, LANES)))
            run_len = blk // len(qruns)
            for i, rs in enumerate(qruns):
                piece = slice(i * run_len, (i + 1) * run_len)
                acc_sc[bi, rs, :] = o_b[piece]
                m_sc[bi, rs, :] = m_b[piece]
                l_sc[bi, rs, :] = l_b[piece]

    blocks = [(d, j) for d in DILATIONS for j in range(n_blocks)]
    groups = [blocks[i:i + ATTN_GROUP] for i in range(0, len(blocks), ATTN_GROUP)]
    scores = scores_of(groups[0])
    for i, group in enumerate(groups):
        nxt = scores_of(groups[i + 1]) if i + 1 < len(groups) else None
        finish(group, scores)
        scores = nxt

    def merge_body(t, carry):
        rows = pl.ds(pl.multiple_of(t * blk, blk), blk)
        ms = [m_sc[bi, rows, :] for bi in range(len(DILATIONS))]
        m_all = functools.reduce(jnp.maximum, ms)
        den = jnp.zeros((blk, LANES), F32)
        num = jnp.zeros((blk, LANES), F32)
        for bi in range(len(DILATIONS)):
            w = jnp.exp2(ms[bi] - m_all)
            den = den + w * pltpu.roll(l_sc[bi, rows, :], ATTN_HEAD_DIM, axis=1)
            num = num + w * acc_sc[bi, rows, :]
        o_ref[rows, :] = num / den
        return carry

    lax.fori_loop(0, n_blocks, merge_body, 0, unroll=16)


def _attention(att_ops, bias):
    B, S, _ = att_ops.shape
    n_pairs = ATTN_WIDTH // LANES
    blockspec = lambda group: pl.BlockSpec((None, S, LANES),
                                           lambda b, p: (b, 0, group * n_pairs + p))
    return pl.pallas_call(
        functools.partial(_attn_kernel, seq=S),
        grid=(B, n_pairs),
        in_specs=[blockspec(group) for group in range(ATT_OPS_WIDTH // ATTN_WIDTH)] + [
            pl.BlockSpec(bias.shape, lambda b, p: (0, 0, 0), pipeline_mode=pl.Buffered(1)),
        ],
        out_specs=pl.BlockSpec((None, S, LANES), lambda b, p: (b, 0, p)),
        out_shape=jax.ShapeDtypeStruct((B, S, ATTN_WIDTH), F32),
        scratch_shapes=[pltpu.VMEM((len(DILATIONS), S, LANES), F32)] * 3,
        compiler_params=pltpu.CompilerParams(
            dimension_semantics=("arbitrary", "arbitrary"),
            vmem_limit_bytes=VMEM_LIMIT),
        name="dilated_attn",
    )(*([att_ops] * (ATT_OPS_WIDTH // ATTN_WIDTH)), bias)


def _hgrn_kernel(q_ref, f_ref, i_ref, g_ref, lbl_ref, gn_ref, wout_ref, wup_ref, wdown_ref,
                 o_ref, wout_bf_ref, wup_bf_ref, wdown_bf_ref, tri_sc, *, seq):
    wout_bf_ref[...] = wout_ref[...].astype(BF16)
    wup_bf_ref[...] = wup_ref[...].astype(BF16)
    wdown_bf_ref[...] = wdown_ref[...].astype(BF16)

    R = HGRN_ROWS
    C = HGRN_CHUNK
    NC = R // C
    G = seq // R
    HS = HGRN_HEADS_PER_STEP
    rows_of = lambda g: slice(g * R, (g + 1) * R)
    f_mid, f_half = [], []
    for hh in range(HS):
        logits = lbl_ref[hh]
        e = jnp.exp(logits - jnp.max(logits, axis=0, keepdims=True))
        lb = e[0:1] / jnp.sum(e, axis=0, keepdims=True)
        f_mid.append(0.5 * (1.0 + lb))
        f_half.append(0.5 * (1.0 - lb))

    row = lax.broadcasted_iota(jnp.int32, (R, R), 0)
    col = lax.broadcasted_iota(jnp.int32, (R, R), 1)
    causal = (row >= col) & ((row // C) == (col // C))
    tri_sc[...] = jnp.where(causal, 1.0, 0.0).astype(BF16)

    ivs, keys, qfs, b2s = {}, {}, {}, {}
    q_decs, k_ends, decays, a_s = {}, {}, {}, {}
    o_intras, u_ts, o_inters = {}, {}, {}
    st = [jnp.zeros((HGRN_DIM, HGRN_DIM), F32) for _ in range(HS)]

    def gates_and_prefix_sum(un):
        hh, g = un
        q = q_ref[hh, rows_of(g), :].astype(F32)
        f = f_ref[hh, rows_of(g), :].astype(F32)
        ivs[un] = i_ref[hh, rows_of(g), :].astype(BF16)
        hq = 0.5 * q
        qfs[un] = hq + hq * jnp.tanh(hq)
        forget = f_mid[hh] + f_half[hh] * jnp.tanh(0.5 * f)
        keys[un] = 1.0 - forget
        lf = jnp.log2(forget)
        lf_hi = lf.astype(BF16)
        lf_lo = (lf - lf_hi.astype(F32)).astype(BF16)
        b2s[un] = jnp.dot(tri_sc[...], jnp.concatenate([lf_hi, lf_lo], axis=1),
                          preferred_element_type=F32)

    def intra_chunk_scores(un):
        b2 = b2s.pop(un)
        b = b2[:, :HGRN_DIM] + b2[:, HGRN_DIM:]
        decay_rows = [jnp.exp2(b[j * C + C - 1:j * C + C]) for j in range(NC)]
        decay_end = jnp.concatenate(
            [jnp.broadcast_to(d, (C, HGRN_DIM)) for d in decay_rows], axis=0)
        k_inv_f = keys.pop(un) * jnp.exp2(-b)
        q_decs[un] = (qfs.pop(un) * jnp.exp2(b)).astype(BF16)
        k_ends[un] = (k_inv_f * decay_end).astype(BF16)
        decays[un] = decay_rows
        a_s[un] = lax.dot_general(q_decs[un], k_inv_f.astype(BF16),
                                  (((1,), (1,)), ((), ())),
                                  preferred_element_type=F32)

    def intra_chunk_outputs(un):
        a = jnp.where(causal, a_s.pop(un), 0.0).astype(BF16)
        iv = ivs.pop(un)
        k_end = k_ends.pop(un)
        o_intras[un] = jnp.dot(a, iv, preferred_element_type=F32)
        u_ts[un] = [lax.dot_general(iv[j * C:(j + 1) * C], k_end[j * C:(j + 1) * C],
                                    (((0,), (0,)), ((), ())),
                                    preferred_element_type=F32)
                    for j in range(NC)]

    def state_scan(un):
        hh, g = un
        parts = []
        q_dec = q_decs.pop(un)
        for j in range(NC):
            parts.append(lax.dot_general(q_dec[j * C:(j + 1) * C], st[hh].astype(BF16),
                                         (((1,), (1,)), ((), ())),
                                         preferred_element_type=F32))
            st[hh] = st[hh] * decays[un][j] + u_ts[un][j]
        o_inters[un] = jnp.concatenate(parts, axis=0)

    def emit(un):
        hh, g = un
        o = _rms(o_intras.pop(un) + o_inters.pop(un), gn_ref[hh])
        hg = 0.5 * g_ref[hh, rows_of(g), :].astype(F32)
        o_ref[hh, rows_of(g), :] = (o * (hg + hg * jnp.tanh(hg))).astype(o_ref.dtype)

    units = [(hh, g) for g in range(G) for hh in range(HS)]
    stages = (gates_and_prefix_sum, intra_chunk_scores, intra_chunk_outputs,
              state_scan, emit)
    for t in range(len(units) + len(stages) - 1):
        for k in reversed(range(len(stages))):
            u = t - k
            if 0 <= u < len(units):
                stages[k](units[u])


def _hgrn(hg, lb_logits, gn, weights):
    _, B, S, _ = hg.shape
    H = HGRN_HEADS
    HS = HGRN_HEADS_PER_STEP
    n_steps = B * (H // HS)
    n_rows = lb_logits.shape[1]
    blockspec = lambda part: pl.BlockSpec((HS, None, S, HGRN_DIM),
                                          lambda b, h: (part * (H // HS) + h, b, 0, 0))
    slabs = [w.reshape(n_steps, w.shape[0] // n_steps, w.shape[1]) for w in weights]
    slab_specs = [pl.BlockSpec((None,) + s.shape[1:], lambda b, h: (b * (H // HS) + h, 0, 0))
                  for s in slabs]
    outs = pl.pallas_call(
        functools.partial(_hgrn_kernel, seq=S),
        grid=(B, H // HS),
        in_specs=[
            blockspec(0), blockspec(1), blockspec(2), blockspec(3),
            pl.BlockSpec((HS, n_rows, HGRN_DIM), lambda b, h: (h, 0, 0)),
            pl.BlockSpec((HS, 1, HGRN_DIM), lambda b, h: (h, 0, 0)),
        ] + slab_specs,
        out_specs=[pl.BlockSpec((HS, None, S, HGRN_DIM), lambda b, h: (h, b, 0, 0))] + slab_specs,
        out_shape=[jax.ShapeDtypeStruct((H, B, S, HGRN_DIM), BF16)]
        + [jax.ShapeDtypeStruct(s.shape, BF16) for s in slabs],
        scratch_shapes=[pltpu.VMEM((HGRN_ROWS, HGRN_ROWS), BF16)],
        compiler_params=pltpu.CompilerParams(
            dimension_semantics=("arbitrary", "arbitrary"),
            vmem_limit_bytes=VMEM_LIMIT),
        name="hgrn2",
    )(hg, hg, hg, hg, lb_logits, gn, *slabs)
    return outs[0], [o.reshape(w.shape) for o, w in zip(outs[1:], weights)]


def _ffn_kernel(x_ref, attn_ref, rec_ref, ga_ref, wout_ref, g2_ref, wup_ref,
                cw_ref, cb_ref, wdown_ref, gf_ref, o_ref,
                h_sc, u_sc, act_sc, carry_sc, nat_sc, nat2_sc, *, tiles_per_seq):
    tm = x_ref.shape[0]
    i = pl.program_id(0)

    @pl.when(i % tiles_per_seq == 0)
    def _():
        carry_sc[...] = jnp.zeros_like(carry_sc)

    an = _rms(attn_ref[...], ga_ref[...])
    run = tm // ATTN_RESIDUES
    quarter = tm // 4
    for s in range(ATTN_WIDTH // LANES):
        for r16 in range(ATTN_RESIDUES):
            nat_sc[s, pl.ds((r16 % 4) * quarter + r16 // 4, run, stride=4), :] = (
                an[r16, :, s * LANES:(s + 1) * LANES])
        for r4 in range(4):
            nat2_sc[s, pl.ds(r4, quarter, stride=4), :] = nat_sc[s, r4 * quarter:(r4 + 1) * quarter, :]
    an = jnp.concatenate([nat2_sc[s] for s in range(ATTN_WIDTH // LANES)], axis=1)
    mix = jnp.dot(an.astype(BF16), wout_ref[:ATTN_WIDTH, :], preferred_element_type=F32)
    rec = jnp.concatenate([rec_ref[h] for h in range(HGRN_HEADS)], axis=1)
    mix = mix + jnp.dot(rec, wout_ref[ATTN_WIDTH:, :],
                        preferred_element_type=F32)
    h = x_ref[...] + mix
    h_sc[...] = h
    u_sc[...] = _rms(h, g2_ref[...]).astype(BF16)

    sqrt_half = math.sqrt(0.5)

    def chunk(start, width):
        u = u_sc[...]
        row = lax.broadcasted_iota(jnp.int32, (tm, width), 0)
        cols = slice(start, start + width)
        vcols = slice(D_FF + start, D_FF + start + width)
        gate = jnp.dot(u, wup_ref[:, cols], preferred_element_type=F32)
        val = jnp.dot(u, wup_ref[:, vcols], preferred_element_type=F32)
        prev = carry_sc[:, cols]
        carry_sc[:, cols] = gate[tm - SUBLANES:, :]
        p1 = jnp.broadcast_to(prev[SUBLANES - 1:SUBLANES], (tm, width))
        p2 = jnp.broadcast_to(prev[SUBLANES - 2:SUBLANES - 1], (tm, width))
        g1 = jnp.where(row == 0, p1, pltpu.roll(gate, 1, axis=0))
        g2 = jnp.where(row == 0, p2, jnp.where(row == 1, p1,
                                               pltpu.roll(gate, 2, axis=0)))
        cw = cw_ref[:, cols]
        conv = cb_ref[:, cols] + cw[0:1] * g2 + cw[1:2] * g1 + cw[2:3] * gate
        act = 0.5 * conv * (1.0 + lax.erf(conv * sqrt_half)) * val
        act_sc[:, cols] = act.astype(BF16)

    start = 0
    for width in FF_CHUNKS:
        chunk(start, width)
        start += width
    down = jnp.dot(act_sc[...], wdown_ref[...], preferred_element_type=F32)
    o_ref[...] = _rms(h_sc[...] + down, gf_ref[...])


def _ffn(x2, attn, rec2, ga, wout, g2, wup, cw, cb, wdown, gf, tm, seq):
    T = x2.shape[0]
    tps = seq // tm
    const = lambda shape: pl.BlockSpec(shape, lambda i: (0,) * len(shape),
                                       pipeline_mode=pl.Buffered(1))
    return pl.pallas_call(
        functools.partial(_ffn_kernel, tiles_per_seq=tps),
        grid=(T // tm,),
        in_specs=[
            pl.BlockSpec((tm, D_MODEL), lambda i: (i, 0)),
            pl.BlockSpec((None, ATTN_RESIDUES, tm // ATTN_RESIDUES, ATTN_WIDTH),
                         lambda i: (i // tps, 0, i % tps, 0)),
            pl.BlockSpec((HGRN_HEADS, tm, HGRN_DIM), lambda i: (0, i, 0)),
            const((1, ATTN_WIDTH)),
            const((D_MODEL, D_MODEL)),
            const((1, D_MODEL)),
            const((D_MODEL, 2 * D_FF)),
            const((CONV_WIDTH, D_FF)),
            const((1, D_FF)),
            const((D_FF, D_MODEL)),
            const((1, D_MODEL)),
        ],
        out_specs=pl.BlockSpec((tm, D_MODEL), lambda i: (i, 0)),
        out_shape=jax.ShapeDtypeStruct((T, D_MODEL), F32),
        scratch_shapes=[
            pltpu.VMEM((tm, D_MODEL), F32),
            pltpu.VMEM((tm, D_MODEL), BF16),
            pltpu.VMEM((tm, D_FF), BF16),
            pltpu.VMEM((SUBLANES, D_FF), F32),
            pltpu.VMEM((ATTN_WIDTH // LANES, tm, LANES), F32),
            pltpu.VMEM((ATTN_WIDTH // LANES, tm, LANES), F32),
        ],
        compiler_params=pltpu.CompilerParams(
            dimension_semantics=("arbitrary",), vmem_limit_bytes=VMEM_LIMIT),
        name="outproj_convglu",
    )(x2, attn, rec2, ga, wout, g2, wup, cw, cb, wdown, gf)


def kernel(x, norm1_g, w_in, attn_norm_g, hgrn_norm_g, hgrn_lb_logits, w_out,
           norm2_g, w_up, conv_w, conv_b, w_down, final_norm_g):
    B, S, D = x.shape
    depth = w_in.shape[0]
    assert depth == 1 and D == D_MODEL
    assert S % (ATTN_BLOCK * DILATIONS[-1]) == 0 and S % HGRN_ROWS == 0
    assert S % PROJ_TILE == 0 and S % FFN_TILE == 0
    T = B * S
    layer = 0
    x2 = x.reshape(T, D)

    att_ops, hg = _inproj(x2, norm1_g[layer].reshape(1, D), w_in[layer],
                          PROJ_TILE, S)

    attn = _attention(att_ops.reshape(B, S, ATT_OPS_WIDTH), _attn_bias())
    attn = attn.reshape(B, ATTN_RESIDUES, S // ATTN_RESIDUES, ATTN_WIDTH)

    lbl = hgrn_lb_logits.astype(F32).reshape(depth + 1, HGRN_HEADS, HGRN_DIM)
    lbl = lbl.transpose(1, 0, 2)
    gn = hgrn_norm_g[layer].reshape(HGRN_HEADS, 1, HGRN_DIM)
    rec, (wout, wup, wdown) = _hgrn(hg.reshape(HG_WIDTH // LANES, B, S, LANES), lbl, gn,
                                    [w_out[layer], w_up[layer], w_down[layer]])

    cw = conv_w[layer]
    cb = conv_b[layer].reshape(1, D_FF)
    out = _ffn(x2, attn, rec.reshape(HGRN_HEADS, T, HGRN_DIM),
               attn_norm_g[layer].reshape(1, ATTN_WIDTH), wout,
               norm2_g[layer].reshape(1, D), wup, cw, cb, wdown,
               final_norm_g.reshape(1, D), FFN_TILE, S)
    return out.reshape(B, S, D)
```

```python
import functools
import math

import jax
import jax.numpy as jnp
import numpy as np
from jax import lax
from jax.experimental import pallas as pl
from jax.experimental.pallas import tpu as pltpu

F32 = jnp.float32
BF16 = jnp.bfloat16

D_MODEL = 1024
ATTN_WIDTH = 512
ATTN_HEADS = 8
ATTN_HEAD_DIM = 64
ATTN_BLOCK = 128
DILATIONS = (1, 4, 16)
ATTN_RESIDUES = 16
HGRN_WIDTH = 512
HGRN_HEADS = 4
HGRN_DIM = 128
HGRN_CHUNK = 64
D_FF = 2816
CONV_WIDTH = 3
NORM_EPS = 1e-6
LOG2_E = math.log2(math.e)
QKV_WIDTH = 3 * ATTN_WIDTH
ATT_OPS_WIDTH = 5 * ATTN_WIDTH
HG_WIDTH = 4 * HGRN_WIDTH

LANES = 128
SUBLANES = 8
MXU_COLS = 256
FF_CHUNKS = (768, 768, 768, 512)
PROJ_TILE = 1024
FFN_TILE = 512
HGRN_ROWS = 256
HGRN_HEADS_PER_STEP = 2
ATTN_GROUP = 16
VMEM_LIMIT = 56 * 1024 * 1024


def _rms(x, g):
    return x * lax.rsqrt(jnp.mean(x * x, axis=-1, keepdims=True) + NORM_EPS) * g


def _inproj_kernel(x_ref, g_ref, w_ref, att_ref, hg_ref, res_sc, tmp_sc):
    tm = x_ref.shape[0]
    run = tm // ATTN_RESIDUES
    quarter = tm // 4
    u32 = _rms(x_ref[...], g_ref[...])
    u = u32.astype(BF16)
    slabs = []
    for sl in range(D_MODEL // LANES):
        res_sc[sl] = u32[:, sl * LANES:(sl + 1) * LANES]
        for r4 in range(4):
            tmp_sc[sl, r4 * quarter:(r4 + 1) * quarter, :] = res_sc[sl, pl.ds(r4, quarter, stride=4), :]
        slabs.append(jnp.concatenate(
            [tmp_sc[sl, pl.ds((r16 % 4) * quarter + r16 // 4, run, stride=4), :]
             for r16 in range(ATTN_RESIDUES)], axis=0))
    u16 = jnp.concatenate(slabs, axis=1).astype(BF16)

    head0 = lax.broadcasted_iota(jnp.int32, (run, LANES), 1) < ATTN_HEAD_DIM
    scale = ATTN_HEAD_DIM ** -0.5 * LOG2_E
    zero = jnp.zeros((run, LANES), F32)
    one = jnp.ones((run, LANES), F32)
    pairs = ATTN_WIDTH // LANES

    def emit_operands(s, slab):
        kind, p = divmod(s, pairs)
        cols = lambda group: slice((group * pairs + p) * LANES, (group * pairs + p + 1) * LANES)
        for r16 in range(ATTN_RESIDUES):
            rows = slab[r16 * run:(r16 + 1) * run]
            if kind == 0:
                q = rows * scale
                att_ref[r16, :, cols(0)] = jnp.where(head0, q, zero).astype(BF16)
                att_ref[r16, :, cols(1)] = jnp.where(head0, zero, q).astype(BF16)
            elif kind == 1:
                att_ref[r16, :, cols(2)] = rows.astype(BF16)
            else:
                att_ref[r16, :, cols(3)] = jnp.where(head0, rows, one).astype(BF16)
                att_ref[r16, :, cols(4)] = jnp.where(head0, one, rows).astype(BF16)

    n_att = QKV_WIDTH // MXU_COLS
    n_gate = HG_WIDTH // MXU_COLS
    for c in range(max(n_att, n_gate)):
        if c < n_gate:
            wcols = slice(QKV_WIDTH + c * MXU_COLS, QKV_WIDTH + (c + 1) * MXU_COLS)
            r = jnp.dot(u, w_ref[:, wcols].astype(BF16),
                        preferred_element_type=F32).astype(hg_ref.dtype)
            hg_ref[2 * c] = r[:, :LANES]
            hg_ref[2 * c + 1] = r[:, LANES:]
        if c < n_att:
            r = jnp.dot(u16, w_ref[:, c * MXU_COLS:(c + 1) * MXU_COLS].astype(BF16),
                        preferred_element_type=F32)
            emit_operands(2 * c, r[:, :LANES])
            emit_operands(2 * c + 1, r[:, LANES:])


def _inproj(x2, g, w, tm, seq):
    T = x2.shape[0]
    tps = seq // tm
    run = tm // ATTN_RESIDUES
    return pl.pallas_call(
        _inproj_kernel,
        grid=(T // tm,),
        in_specs=[
            pl.BlockSpec((tm, D_MODEL), lambda i: (i, 0)),
            pl.BlockSpec((1, D_MODEL), lambda i: (0, 0)),
            pl.BlockSpec((D_MODEL, QKV_WIDTH + HG_WIDTH), lambda i: (0, 0),
                         pipeline_mode=pl.Buffered(1)),
        ],
        out_specs=[
            pl.BlockSpec((None, ATTN_RESIDUES, run, ATT_OPS_WIDTH),
                         lambda i: (i // tps, 0, i % tps, 0)),
            pl.BlockSpec((HG_WIDTH // LANES, tm, LANES), lambda i: (0, i, 0)),
        ],
        out_shape=[
            jax.ShapeDtypeStruct((T // seq, ATTN_RESIDUES, seq // ATTN_RESIDUES,
                                  ATT_OPS_WIDTH), BF16),
            jax.ShapeDtypeStruct((HG_WIDTH // LANES, T, LANES), BF16),
        ],
        scratch_shapes=[pltpu.VMEM((D_MODEL // LANES, tm, LANES), F32),
                        pltpu.VMEM((D_MODEL // LANES, tm + SUBLANES, LANES), F32)],
        compiler_params=pltpu.CompilerParams(
            dimension_semantics=("arbitrary",), vmem_limit_bytes=VMEM_LIMIT),
        name="inproj",
    )(x2, g, w)


def _attn_runs(d):
    g = ATTN_RESIDUES // d
    return g, ATTN_BLOCK // g


def _attn_bias():
    blk = ATTN_BLOCK
    e = np.arange(2 * blk) % blk
    half = np.arange(2 * blk) // blk
    out = []
    for d in DILATIONS:
        g, run = _attn_runs(d)
        pos = g * (e % run) + e // run
        pos_q = pos[:, None]
        pos_k = (blk * half + pos)[None, :]
        dist = blk + pos_q - pos_k
        out.append((dist >= 0) & (dist <= blk))
        out.append(pos_k <= pos_q)
    return np.where(np.stack(out), 0.0, -np.inf).astype(np.float32)


def _attn_kernel(q0_ref, q1_ref, k_ref, v0_ref, v1_ref, bias_ref, o_ref,
                 m_sc, l_sc, acc_sc, *, seq):
    blk = ATTN_BLOCK
    n_blocks = seq // blk
    ns = seq // ATTN_RESIDUES
    head0 = lax.broadcasted_iota(jnp.int32, (blk, LANES), 1) < ATTN_HEAD_DIM

    def gather(ref, runs):
        return jnp.concatenate([ref[rs, :] for rs in runs], axis=0)

    def block_runs(d, j, which):
        nb = n_blocks // d
        g, run = _attn_runs(d)
        r, n = divmod(j, nb)
        n0 = max(n - 1, 0)
        nsel = {"q": [n], "kv": [n0, n0 + 1]}[which]
        return [pl.ds((d * jj + r) * ns + run * nn, run) for nn in nsel for jj in range(g)]

    def scores_of(group):
        out = []
        for d, j in group:
            qruns = block_runs(d, j, "q")
            qs = jnp.concatenate([gather(q0_ref, qruns),
                                  gather(q1_ref, qruns)], axis=0)
            kw = gather(k_ref, block_runs(d, j, "kv"))
            out.append(lax.dot_general(qs, kw, (((1,), (1,)), ((), ())),
                                       preferred_element_type=F32))
        return out

    def finish(group, scores):
        parts = []
        for (d, j), s in zip(group, scores):
            kruns = block_runs(d, j, "kv")
            first = 1 if j % (n_blocks // d) == 0 else 0
            s = s + bias_ref[2 * DILATIONS.index(d) + first]
            m = jnp.max(s, axis=-1, keepdims=True)
            p = jnp.exp2(s - m).astype(BF16)
            pv0 = jnp.dot(p[:blk], gather(v0_ref, kruns),
                          preferred_element_type=F32)
            pv1 = jnp.dot(p[blk:], gather(v1_ref, kruns),
                          preferred_element_type=F32)
            parts.append((pv0, pv1, m))
        for (d, j), (pv0, pv1, m) in zip(group, parts):
            bi = DILATIONS.index(d)
            qruns = block_runs(d, j, "q")
            o_b = jnp.where(head0, pv0, pv1)
            l_b = jnp.where(head0, pv1, pv0)
            m_b = jnp.where(head0, jnp.broadcast_to(m[:blk], (blk, LANES)),
                            jnp.broadcast_to(m[blk:], (blk, LANES)))
            run_len = blk // len(qruns)
            for i, rs in enumerate(qruns):
                piece = slice(i * run_len, (i + 1) * run_len)
                acc_sc[bi, rs, :] = o_b[piece]
                m_sc[bi, rs, :] = m_b[piece]
                l_sc[bi, rs, :] = l_b[piece]

    blocks = [(d, j) for d in DILATIONS for j in range(n_blocks)]
    groups = [blocks[i:i + ATTN_GROUP] for i in range(0, len(blocks), ATTN_GROUP)]
    scores = scores_of(groups[0])
    for i, group in enumerate(groups):
        nxt = scores_of(groups[i + 1]) if i + 1 < len(groups) else None
        finish(group, scores)
        scores = nxt

    def merge_body(t, carry):
        rows = pl.ds(pl.multiple_of(t * blk, blk), blk)
        ms = [m_sc[bi, rows, :] for bi in range(len(DILATIONS))]
        m_all = functools.reduce(jnp.maximum, ms)
        den = jnp.zeros((blk, LANES), F32)
        num = jnp.zeros((blk, LANES), F32)
        for bi in range(len(DILATIONS)):
            w = jnp.exp2(ms[bi] - m_all)
            den = den + w * pltpu.roll(l_sc[bi, rows, :], ATTN_HEAD_DIM, axis=1)
            num = num + w * acc_sc[bi, rows, :]
        o_ref[rows, :] = num / den
        return carry

    lax.fori_loop(0, n_blocks, merge_body, 0, unroll=16)


def _attention(att_ops, bias):
    B, S, _ = att_ops.shape
    n_pairs = ATTN_WIDTH // LANES
    blockspec = lambda group: pl.BlockSpec((None, S, LANES),
                                           lambda b, p: (b, 0, group * n_pairs + p))
    return pl.pallas_call(
        functools.partial(_attn_kernel, seq=S),
        grid=(B, n_pairs),
        in_specs=[blockspec(group) for group in range(ATT_OPS_WIDTH // ATTN_WIDTH)] + [
            pl.BlockSpec(bias.shape, lambda b, p: (0, 0, 0), pipeline_mode=pl.Buffered(1)),
        ],
        out_specs=pl.BlockSpec((None, S, LANES), lambda b, p: (b, 0, p)),
        out_shape=jax.ShapeDtypeStruct((B, S, ATTN_WIDTH), F32),
        scratch_shapes=[pltpu.VMEM((len(DILATIONS), S, LANES), F32)] * 3,
        compiler_params=pltpu.CompilerParams(
            dimension_semantics=("arbitrary", "arbitrary"),
            vmem_limit_bytes=VMEM_LIMIT),
        name="dilated_attn",
    )(*([att_ops] * (ATT_OPS_WIDTH // ATTN_WIDTH)), bias)


def _hgrn_kernel(q_ref, f_ref, i_ref, g_ref, lbl_ref, gn_ref, wout_ref, wup_ref, wdown_ref,
                 o_ref, wout_bf_ref, wup_bf_ref, wdown_bf_ref, tri_sc, *, seq):
    wout_bf_ref[...] = wout_ref[...].astype(BF16)
    wup_bf_ref[...] = wup_ref[...].astype(BF16)
    wdown_bf_ref[...] = wdown_ref[...].astype(BF16)

    R = HGRN_ROWS
    C = HGRN_CHUNK
    NC = R // C
    G = seq // R
    HS = HGRN_HEADS_PER_STEP
    rows_of = lambda g: slice(g * R, (g + 1) * R)
    f_mid, f_half = [], []
    for hh in range(HS):
        logits = lbl_ref[hh]
        e = jnp.exp(logits - jnp.max(logits, axis=0, keepdims=True))
        lb = e[0:1] / jnp.sum(e, axis=0, keepdims=True)
        f_mid.append(0.5 * (1.0 + lb))
        f_half.append(0.5 * (1.0 - lb))

    row = lax.broadcasted_iota(jnp.int32, (R, R), 0)
    col = lax.broadcasted_iota(jnp.int32, (R, R), 1)
    causal = (row >= col) & ((row // C) == (col // C))
    tri_sc[...] = jnp.where(causal, 1.0, 0.0).astype(BF16)

    ivs, keys, qfs, b2s = {}, {}, {}, {}
    q_decs, k_ends, decays, a_s = {}, {}, {}, {}
    o_intras, u_ts, o_inters = {}, {}, {}
    st = [jnp.zeros((HGRN_DIM, HGRN_DIM), F32) for _ in range(HS)]

    def gates_and_prefix_sum(un):
        hh, g = un
        q = q_ref[hh, rows_of(g), :].astype(F32)
        f = f_ref[hh, rows_of(g), :].astype(F32)
        ivs[un] = i_ref[hh, rows_of(g), :].astype(BF16)
        hq = 0.5 * q
        qfs[un] = hq + hq * jnp.tanh(hq)
        forget = f_mid[hh] + f_half[hh] * jnp.tanh(0.5 * f)
        keys[un] = 1.0 - forget
        lf = jnp.log2(forget)
        lf_hi = lf.astype(BF16)
        lf_lo = (lf - lf_hi.astype(F32)).astype(BF16)
        b2s[un] = jnp.dot(tri_sc[...], jnp.concatenate([lf_hi, lf_lo], axis=1),
                          preferred_element_type=F32)

    def intra_chunk_scores(un):
        b2 = b2s.pop(un)
        b = b2[:, :HGRN_DIM] + b2[:, HGRN_DIM:]
        decay_rows = [jnp.exp2(b[j * C + C - 1:j * C + C]) for j in range(NC)]
        decay_end = jnp.concatenate(
            [jnp.broadcast_to(d, (C, HGRN_DIM)) for d in decay_rows], axis=0)
        k_inv_f = keys.pop(un) * jnp.exp2(-b)
        q_decs[un] = (qfs.pop(un) * jnp.exp2(b)).astype(BF16)
        k_ends[un] = (k_inv_f * decay_end).astype(BF16)
        decays[un] = decay_rows
        a_s[un] = lax.dot_general(q_decs[un], k_inv_f.astype(BF16),
                                  (((1,), (1,)), ((), ())),
                                  preferred_element_type=F32)

    def intra_chunk_outputs(un):
        a = jnp.where(causal, a_s.pop(un), 0.0).astype(BF16)
        iv = ivs.pop(un)
        k_end = k_ends.pop(un)
        o_intras[un] = jnp.dot(a, iv, preferred_element_type=F32)
        u_ts[un] = [lax.dot_general(iv[j * C:(j + 1) * C], k_end[j * C:(j + 1) * C],
                                    (((0,), (0,)), ((), ())),
                                    preferred_element_type=F32)
                    for j in range(NC)]

    def state_scan(un):
        hh, g = un
        parts = []
        q_dec = q_decs.pop(un)
        for j in range(NC):
            parts.append(lax.dot_general(q_dec[j * C:(j + 1) * C], st[hh].astype(BF16),
                                         (((1,), (1,)), ((), ())),
                                         preferred_element_type=F32))
            st[hh] = st[hh] * decays[un][j] + u_ts[un][j]
        o_inters[un] = jnp.concatenate(parts, axis=0)

    def emit(un):
        hh, g = un
        o = _rms(o_intras.pop(un) + o_inters.pop(un), gn_ref[hh])
        hg = 0.5 * g_ref[hh, rows_of(g), :].astype(F32)
        o_ref[hh, rows_of(g), :] = (o * (hg + hg * jnp.tanh(hg))).astype(o_ref.dtype)

    units = [(hh, g) for g in range(G) for hh in range(HS)]
    stages = (gates_and_prefix_sum, intra_chunk_scores, intra_chunk_outputs,
              state_scan, emit)
    for t in range(len(units) + len(stages) - 1):
        for k in reversed(range(len(stages))):
            u = t - k
            if 0 <= u < len(units):
                stages[k](units[u])


def _hgrn(hg, lb_logits, gn, weights):
    _, B, S, _ = hg.shape
    H = HGRN_HEADS
    HS = HGRN_HEADS_PER_STEP
    n_steps = B * (H // HS)
    n_rows = lb_logits.shape[1]
    blockspec = lambda part: pl.BlockSpec((HS, None, S, HGRN_DIM),
                                          lambda b, h: (part * (H // HS) + h, b, 0, 0))
    slabs = [w.reshape(n_steps, w.shape[0] // n_steps, w.shape[1]) for w in weights]
    slab_specs = [pl.BlockSpec((None,) + s.shape[1:], lambda b, h: (b * (H // HS) + h, 0, 0))
                  for s in slabs]
    outs = pl.pallas_call(
        functools.partial(_hgrn_kernel, seq=S),
        grid=(B, H // HS),
        in_specs=[
            blockspec(0), blockspec(1), blockspec(2), blockspec(3),
            pl.BlockSpec((HS, n_rows, HGRN_DIM), lambda b, h: (h, 0, 0)),
            pl.BlockSpec((HS, 1, HGRN_DIM), lambda b, h: (h, 0, 0)),
        ] + slab_specs,
        out_specs=[pl.BlockSpec((HS, None, S, HGRN_DIM), lambda b, h: (h, b, 0, 0))] + slab_specs,
        out_shape=[jax.ShapeDtypeStruct((H, B, S, HGRN_DIM), BF16)]
        + [jax.ShapeDtypeStruct(s.shape, BF16) for s in slabs],
        scratch_shapes=[pltpu.VMEM((HGRN_ROWS, HGRN_ROWS), BF16)],
        compiler_params=pltpu.CompilerParams(
            dimension_semantics=("arbitrary", "arbitrary"),
            vmem_limit_bytes=VMEM_LIMIT),
        name="hgrn2",
    )(hg, hg, hg, hg, lb_logits, gn, *slabs)
    return outs[0], [o.reshape(w.shape) for o, w in zip(outs[1:], weights)]


def _ffn_kernel(x_ref, attn_ref, rec_ref, ga_ref, wout_ref, g2_ref, wup_ref,
                cw_ref, cb_ref, wdown_ref, gf_ref, o_ref,
                h_sc, u_sc, act_sc, carry_sc, nat_sc, nat2_sc, *, tiles_per_seq):
    tm = x_ref.shape[0]
    i = pl.program_id(0)

    @pl.when(i % tiles_per_seq == 0)
    def _():
        carry_sc[...] = jnp.zeros_like(carry_sc)

    an = _rms(attn_ref[...], ga_ref[...])
    run = tm // ATTN_RESIDUES
    quarter = tm // 4
    for s in range(ATTN_WIDTH // LANES):
        for r16 in range(ATTN_RESIDUES):
            nat_sc[s, pl.ds((r16 % 4) * quarter + r16 // 4, run, stride=4), :] = (
                an[r16, :, s * LANES:(s + 1) * LANES])
        for r4 in range(4):
            nat2_sc[s, pl.ds(r4, quarter, stride=4), :] = nat_sc[s, r4 * quarter:(r4 + 1) * quarter, :]
    an = jnp.concatenate([nat2_sc[s] for s in range(ATTN_WIDTH // LANES)], axis=1)
    mix = jnp.dot(an.astype(BF16), wout_ref[:ATTN_WIDTH, :], preferred_element_type=F32)
    rec = jnp.concatenate([rec_ref[h] for h in range(HGRN_HEADS)], axis=1)
    mix = mix + jnp.dot(rec, wout_ref[ATTN_WIDTH:, :],
                        preferred_element_type=F32)
    h = x_ref[...] + mix
    h_sc[...] = h
    u_sc[...] = _rms(h, g2_ref[...]).astype(BF16)

    sqrt_half = math.sqrt(0.5)

    def chunk(start, width):
        u = u_sc[...]
        row = lax.broadcasted_iota(jnp.int32, (tm, width), 0)
        cols = slice(start, start + width)
        vcols = slice(D_FF + start, D_FF + start + width)
        gate = jnp.dot(u, wup_ref[:, cols], preferred_element_type=F32)
        val = jnp.dot(u, wup_ref[:, vcols], preferred_element_type=F32)
        prev = carry_sc[:, cols]
        carry_sc[:, cols] = gate[tm - SUBLANES:, :]
        p1 = jnp.broadcast_to(prev[SUBLANES - 1:SUBLANES], (tm, width))
        p2 = jnp.broadcast_to(prev[SUBLANES - 2:SUBLANES - 1], (tm, width))
        g1 = jnp.where(row == 0, p1, pltpu.roll(gate, 1, axis=0))
        g2 = jnp.where(row == 0, p2, jnp.where(row == 1, p1,
                                               pltpu.roll(gate, 2, axis=0)))
        cw = cw_ref[:, cols]
        conv = cb_ref[:, cols] + cw[0:1] * g2 + cw[1:2] * g1 + cw[2:3] * gate
        act = 0.5 * conv * (1.0 + lax.erf(conv * sqrt_half)) * val
        act_sc[:, cols] = act.astype(BF16)

    start = 0
    for width in FF_CHUNKS:
        chunk(start, width)
        start += width
    down = jnp.dot(act_sc[...], wdown_ref[...], preferred_element_type=F32)
    o_ref[...] = _rms(h_sc[...] + down, gf_ref[...])


def _ffn(x2, attn, rec2, ga, wout, g2, wup, cw, cb, wdown, gf, tm, seq):
    T = x2.shape[0]
    tps = seq // tm
    const = lambda shape: pl.BlockSpec(shape, lambda i: (0,) * len(shape),
                                       pipeline_mode=pl.Buffered(1))
    return pl.pallas_call(
        functools.partial(_ffn_kernel, tiles_per_seq=tps),
        grid=(T // tm,),
        in_specs=[
            pl.BlockSpec((tm, D_MODEL), lambda i: (i, 0)),
            pl.BlockSpec((None, ATTN_RESIDUES, tm // ATTN_RESIDUES, ATTN_WIDTH),
                         lambda i: (i // tps, 0, i % tps, 0)),
            pl.BlockSpec((HGRN_HEADS, tm, HGRN_DIM), lambda i: (0, i, 0)),
            const((1, ATTN_WIDTH)),
            const((D_MODEL, D_MODEL)),
            const((1, D_MODEL)),
            const((D_MODEL, 2 * D_FF)),
            const((CONV_WIDTH, D_FF)),
            const((1, D_FF)),
            const((D_FF, D_MODEL)),
            const((1, D_MODEL)),
        ],
        out_specs=pl.BlockSpec((tm, D_MODEL), lambda i: (i, 0)),
        out_shape=jax.ShapeDtypeStruct((T, D_MODEL), F32),
        scratch_shapes=[
            pltpu.VMEM((tm, D_MODEL), F32),
            pltpu.VMEM((tm, D_MODEL), BF16),
            pltpu.VMEM((tm, D_FF), BF16),
            pltpu.VMEM((SUBLANES, D_FF), F32),
            pltpu.VMEM((ATTN_WIDTH // LANES, tm, LANES), F32),
            pltpu.VMEM((ATTN_WIDTH // LANES, tm, LANES), F32),
        ],
        compiler_params=pltpu.CompilerParams(
            dimension_semantics=("arbitrary",), vmem_limit_bytes=VMEM_LIMIT),
        name="outproj_convglu",
    )(x2, attn, rec2, ga, wout, g2, wup, cw, cb, wdown, gf)


def kernel(x, norm1_g, w_in, attn_norm_g, hgrn_norm_g, hgrn_lb_logits, w_out,
           norm2_g, w_up, conv_w, conv_b, w_down, final_norm_g):
    B, S, D = x.shape
    depth = w_in.shape[0]
    assert depth == 1 and D == D_MODEL
    assert S % (ATTN_BLOCK * DILATIONS[-1]) == 0 and S % HGRN_ROWS == 0
    assert S % PROJ_TILE == 0 and S % FFN_TILE == 0
    T = B * S
    layer = 0
    x2 = x.reshape(T, D)

    att_ops, hg = _inproj(x2, norm1_g[layer].reshape(1, D), w_in[layer],
                          PROJ_TILE, S)

    attn = _attention(att_ops.reshape(B, S, ATT_OPS_WIDTH), _attn_bias())
    attn = attn.reshape(B, ATTN_RESIDUES, S // ATTN_RESIDUES, ATTN_WIDTH)

    lbl = hgrn_lb_logits.astype(F32).reshape(depth + 1, HGRN_HEADS, HGRN_DIM)
    lbl = lbl.transpose(1, 0, 2)
    gn = hgrn_norm_g[layer].reshape(HGRN_HEADS, 1, HGRN_DIM)
    rec, (wout, wup, wdown) = _hgrn(hg.reshape(HG_WIDTH // LANES, B, S, LANES), lbl, gn,
                                    [w_out[layer], w_up[layer], w_down[layer]])

    cw = conv_w[layer]
    cb = conv_b[layer].reshape(1, D_FF)
    out = _ffn(x2, attn, rec.reshape(HGRN_HEADS, T, HGRN_DIM),
               attn_norm_g[layer].reshape(1, ATTN_WIDTH), wout,
               norm2_g[layer].reshape(1, D), wup, cw, cb, wdown,
               final_norm_g.reshape(1, D), FFN_TILE, S)
    return out.reshape(B, S, D)
```
